```python
import math
import jax, jax.numpy as jnp
from jax import lax
import numpy as np

D_MODEL = 2048
BATCH = 8
SEQ = 4096
DEPTH = 2

GRID_W = 64
CTX_LEN = 256
BLK = 128
ROPE_BASE = 10000.0
EPS = 1e-6

MLA_HEADS = 6
MLA_Q_RANK = 512
MLA_KV_RANK = 256
MLA_NOPE = 128
MLA_ROPE = 64
MLA_V = 128
MLA_QK = MLA_NOPE + MLA_ROPE
MLA_WIDTH = MLA_HEADS * MLA_V

SWA_Q_HEADS = 6
SWA_KV_HEADS = 2
SWA_GROUP = SWA_Q_HEADS // SWA_KV_HEADS
SWA_HEAD_DIM = 128
SWA_WINDOW = 128
SWA_WIDTH = SWA_Q_HEADS * SWA_HEAD_DIM

DIFF_HEADS = 4
DIFF_QK_DIM = 64
DIFF_V_DIM = 2 * DIFF_QK_DIM
DIFF_WIDTH = DIFF_HEADS * DIFF_V_DIM

MIX_WIDTH = MLA_WIDTH + SWA_WIDTH + DIFF_WIDTH

IN_LAYOUT = (
    ("mla_cq", MLA_Q_RANK), ("mla_ckv", MLA_KV_RANK), ("mla_kr", MLA_ROPE), ("mla_gate", MLA_WIDTH),
    ("swa_q", SWA_WIDTH), ("swa_k", SWA_KV_HEADS * SWA_HEAD_DIM), ("swa_v", SWA_KV_HEADS * SWA_HEAD_DIM), ("swa_gate", SWA_WIDTH),
    ("dif_q", 2 * DIFF_HEADS * DIFF_QK_DIM), ("dif_k", 2 * DIFF_HEADS * DIFF_QK_DIM), ("dif_v", DIFF_WIDTH), ("dif_gate", DIFF_WIDTH),
)
IN_COLS = sum(n for _, n in IN_LAYOUT)

kernel_name = "hybrid_mla_swa_diffattn_prefix_dit"

F32 = jnp.float32


def rms_norm(x, g):
    xf = x.astype(F32)
    y = xf * lax.rsqrt(jnp.mean(xf * xf, axis=-1, keepdims=True) + EPS)
    return (y * g.astype(F32)).astype(x.dtype)


def split_cols(p):
    idx = np.cumsum([n for _, n in IN_LAYOUT])[:-1].tolist()
    parts = jnp.split(p, idx, axis=-1)
    return {name: t for (name, _), t in zip(IN_LAYOUT, parts)}


def adaln(cvec, w, b):
    m = jax.nn.silu(cvec) @ w + b
    return jnp.split(m, 3, axis=-1)


def axial_rope_tables(n_tokens, rot_dim):
    rows = n_tokens // GRID_W
    t_row = jnp.repeat(jnp.arange(rows), GRID_W).astype(F32)
    t_col = jnp.tile(jnp.arange(GRID_W), rows).astype(F32)
    n_freq = rot_dim // 4
    inv = jnp.power(ROPE_BASE, -jnp.arange(n_freq, dtype=F32) / n_freq)
    ang = jnp.concatenate([t_row[:, None] * inv, t_col[:, None] * inv], axis=-1)
    return jnp.cos(ang), jnp.sin(ang)


def apply_rope(x, cos, sin):
    half = x.shape[-1] // 2
    shp = (cos.shape[0],) + (1,) * (x.ndim - 3) + (half,)
    c = cos.reshape(shp).astype(x.dtype)
    s = sin.reshape(shp).astype(x.dtype)
    x1, x2 = x[..., :half], x[..., half:]
    return jnp.concatenate([x1 * c - x2 * s, x1 * s + x2 * c], axis=-1)


def rope_tail(x, cos, sin, r):
    return jnp.concatenate([x[..., :-r], apply_rope(x[..., -r:], cos, sin)], axis=-1)


def block_sweep(fn, q):
    b, s = q.shape[:2]
    nb = s // BLK
    qb = jnp.moveaxis(q.reshape((b, nb, BLK) + q.shape[2:]), 1, 0)
    out = jnp.moveaxis(lax.map(fn, qb), 0, 1)
    return out.reshape((b, s) + out.shape[3:])


def attend(q, k, v, scale):
    s = jnp.einsum('bqhd,bthd->bhqt', q, k).astype(F32) * scale
    p = jax.nn.softmax(s, axis=-1)
    return jnp.einsum('bhqt,bthd->bqhd', p.astype(v.dtype), v)


def mla_queries(cq, q_norm_g, w_uq, q_gain):
    q = (rms_norm(cq, q_norm_g) @ w_uq).reshape(cq.shape[0], cq.shape[1], MLA_HEADS, MLA_QK)
    return rms_norm(q, q_gain)


def mla_keys_values(ckv, kr, kv_norm_g, w_ukv, k_gain):
    kv = (rms_norm(ckv, kv_norm_g) @ w_ukv).reshape(ckv.shape[0], ckv.shape[1], MLA_HEADS, MLA_NOPE + MLA_V)
    k_nope, v = kv[..., :MLA_NOPE], kv[..., MLA_NOPE:]
    k_pe = jnp.broadcast_to(kr[:, :, None, :], k_nope.shape[:3] + (MLA_ROPE,))
    k = rms_norm(jnp.concatenate([k_nope, k_pe], axis=-1), k_gain)
    return k, v


def swa_latent(q, k, v, kc, vc, sink):
    b, s = q.shape[:2]
    nb = s // BLK
    scale = SWA_HEAD_DIM ** -0.5
    qb = q.reshape(b, nb, BLK, SWA_KV_HEADS, SWA_GROUP, SWA_HEAD_DIM)

    def band(t):
        tp = jnp.pad(t, ((0, 0), (BLK, BLK), (0, 0), (0, 0))).reshape(b, nb + 2, BLK, SWA_KV_HEADS, SWA_HEAD_DIM)
        return jnp.concatenate([tp[:, :-2], tp[:, 1:-1], tp[:, 2:]], axis=2)

    kw, vw = band(k), band(v)
    rel = jnp.arange(3 * BLK)[None, :] - BLK - jnp.arange(BLK)[:, None]
    kpos = jnp.arange(nb)[:, None] * BLK + jnp.arange(3 * BLK)[None, :] - BLK
    mask = (jnp.abs(rel) <= SWA_WINDOW)[None] & ((kpos >= 0) & (kpos < s))[:, None, :]

    s_w = jnp.einsum('bnqhgd,bnkhd->bnhgqk', qb, kw).astype(F32) * scale
    s_w = jnp.where(mask[None, :, None, None], s_w, -jnp.inf)
    s_c = jnp.einsum('bnqhgd,bchd->bnhgqc', qb, kc).astype(F32) * scale
    s_k = jnp.broadcast_to(sink.reshape(SWA_KV_HEADS, SWA_GROUP, 1, 1).astype(F32), s_c.shape[:-1] + (1,))
    p = jax.nn.softmax(jnp.concatenate([s_w, s_c, s_k], axis=-1), axis=-1)
    n_w, n_c = 3 * BLK, kc.shape[1]
    p_w = p[..., :n_w].astype(v.dtype)
    p_c = p[..., n_w:n_w + n_c].astype(v.dtype)
    o = jnp.einsum('bnhgqk,bnkhd->bnqhgd', p_w, vw) + jnp.einsum('bnhgqc,bchd->bnqhgd', p_c, vc)
    return o.reshape(b, s, SWA_WIDTH)


def swa_context(q, k, v, sink):
    b, lc = q.shape[:2]
    s = jnp.einsum('bqhgd,bkhd->bhgqk', q, k).astype(F32) * (SWA_HEAD_DIM ** -0.5)
    s_k = jnp.broadcast_to(sink.reshape(SWA_KV_HEADS, SWA_GROUP, 1, 1).astype(F32), s.shape[:-1] + (1,))
    p = jax.nn.softmax(jnp.concatenate([s, s_k], axis=-1), axis=-1)[..., :-1]
    o = jnp.einsum('bhgqk,bkhd->bqhgd', p.astype(v.dtype), v)
    return o.reshape(b, lc, SWA_WIDTH)


def diff_attend(q, k, v, lam):
    s = jnp.einsum('bqhcd,bthcd->bhcqt', q, k).astype(F32) * (DIFF_QK_DIM ** -0.5)
    p = jax.nn.softmax(s, axis=-1)
    a = p[:, :, 0] - lam * p[:, :, 1]
    return jnp.einsum('bhqt,bthd->bqhd', a.astype(v.dtype), v)


def setup_inputs(seed: int = 0) -> dict:
    key = jax.random.key(seed)
    ks = jax.random.split(key, 25)
    L = DEPTH

    def nrm(k, shape, scale):
        return jax.random.normal(k, shape, F32) * scale

    def gain(k, shape):
        return 1.0 + 0.01 * jax.random.normal(k, shape, F32)

    return {
        "x": nrm(ks[0], (BATCH, SEQ, D_MODEL), 1.0),
        "c": nrm(ks[1], (BATCH, D_MODEL), 1.0),
        "ctx": nrm(ks[2], (BATCH, CTX_LEN, D_MODEL), 1.0),
        "c_ctx": nrm(ks[3], (D_MODEL,), 1.0),
        "norm_g": gain(ks[4], (L, D_MODEL)),
        "w_ada": nrm(ks[5], (L, D_MODEL, 3 * D_MODEL), 0.5 * D_MODEL ** -0.5),
        "b_ada": nrm(ks[6], (L, 3 * D_MODEL), 0.01),
        "w_in": nrm(ks[7], (L, D_MODEL, IN_COLS), D_MODEL ** -0.5),
        "mla_q_norm": gain(ks[8], (L, MLA_Q_RANK)),
        "mla_w_uq": nrm(ks[9], (L, MLA_Q_RANK, MLA_HEADS * MLA_QK), MLA_Q_RANK ** -0.5),
        "mla_kv_norm": gain(ks[10], (L, MLA_KV_RANK)),
        "mla_w_ukv": nrm(ks[11], (L, MLA_KV_RANK, MLA_HEADS * (MLA_NOPE + MLA_V)), MLA_KV_RANK ** -0.5),
        "mla_q_gain": gain(ks[12], (L, MLA_QK)),
        "mla_k_gain": gain(ks[13], (L, MLA_QK)),
        "swa_q_gain": gain(ks[14], (L, SWA_HEAD_DIM)),
        "swa_k_gain": gain(ks[15], (L, SWA_HEAD_DIM)),
        "swa_sink": nrm(ks[16], (L, SWA_Q_HEADS), 0.5),
        "dif_q_gain": gain(ks[17], (L, DIFF_QK_DIM)),
        "dif_k_gain": gain(ks[18], (L, DIFF_QK_DIM)),
        "dif_lq1": nrm(ks[19], (L, DIFF_QK_DIM), 0.1),
        "dif_lk1": nrm(ks[20], (L, DIFF_QK_DIM), 0.1),
        "dif_lq2": nrm(ks[21], (L, DIFF_QK_DIM), 0.1),
        "dif_lk2": nrm(ks[22], (L, DIFF_QK_DIM), 0.1),
        "dif_out_gain": gain(ks[23], (L, DIFF_V_DIM)),
        "w_out": nrm(ks[24], (L, MIX_WIDTH, D_MODEL), MIX_WIDTH ** -0.5),
    }


def reference(x, c, ctx, c_ctx, norm_g, w_ada, b_ada, w_in, mla_q_norm, mla_w_uq, mla_kv_norm, mla_w_ukv,
              mla_q_gain, mla_k_gain, swa_q_gain, swa_k_gain, swa_sink, dif_q_gain, dif_k_gain,
              dif_lq1, dif_lk1, dif_lq2, dif_lk2, dif_out_gain, w_out):
    b, s, _ = x.shape
    lc = ctx.shape[1]
    cos64, sin64 = axial_rope_tables(s, MLA_ROPE)
    cos128, sin128 = axial_rope_tables(s, SWA_HEAD_DIM)
    hx, hc = x, ctx

    for l in range(DEPTH):
        need_ctx_out = l < DEPTH - 1
        sh_x, sc_x, g_x = [t[:, None, :] for t in adaln(c, w_ada[l], b_ada[l])]
        sh_c, sc_c, g_c = adaln(c_ctx, w_ada[l], b_ada[l])
        nx = rms_norm(hx, norm_g[l]) * (1.0 + sc_x) + sh_x
        nc = rms_norm(hc, norm_g[l]) * (1.0 + sc_c) + sh_c
        px = split_cols(nx @ w_in[l])
        pc = split_cols(nc @ w_in[l])

        qa_x = rope_tail(mla_queries(px["mla_cq"], mla_q_norm[l], mla_w_uq[l], mla_q_gain[l]), cos64, sin64, MLA_ROPE)
        ka_x, va_x = mla_keys_values(px["mla_ckv"], px["mla_kr"], mla_kv_norm[l], mla_w_ukv[l], mla_k_gain[l])
        ka_x = rope_tail(ka_x, cos64, sin64, MLA_ROPE)
        ka_c, va_c = mla_keys_values(pc["mla_ckv"], pc["mla_kr"], mla_kv_norm[l], mla_w_ukv[l], mla_k_gain[l])
        ka_all = jnp.concatenate([ka_x, ka_c], axis=1)
        va_all = jnp.concatenate([va_x, va_c], axis=1)
        ya_x = block_sweep(lambda qb: attend(qb, ka_all, va_all, MLA_QK ** -0.5), qa_x).reshape(b, s, MLA_WIDTH)

        qb_x = apply_rope(rms_norm(px["swa_q"].reshape(b, s, SWA_KV_HEADS, SWA_GROUP, SWA_HEAD_DIM), swa_q_gain[l]), cos128, sin128)
        kb_x = apply_rope(rms_norm(px["swa_k"].reshape(b, s, SWA_KV_HEADS, SWA_HEAD_DIM), swa_k_gain[l]), cos128, sin128)
        vb_x = px["swa_v"].reshape(b, s, SWA_KV_HEADS, SWA_HEAD_DIM)
        kb_c = rms_norm(pc["swa_k"].reshape(b, lc, SWA_KV_HEADS, SWA_HEAD_DIM), swa_k_gain[l])
        vb_c = pc["swa_v"].reshape(b, lc, SWA_KV_HEADS, SWA_HEAD_DIM)
        yb_x = swa_latent(qb_x, kb_x, vb_x, kb_c, vb_c, swa_sink[l])

        lam_init = 0.8 - 0.6 * math.exp(-0.3 * l)
        lam = (jnp.exp(jnp.sum(dif_lq1[l].astype(F32) * dif_lk1[l].astype(F32)))
               - jnp.exp(jnp.sum(dif_lq2[l].astype(F32) * dif_lk2[l].astype(F32))) + lam_init)
        qc_x = apply_rope(rms_norm(px["dif_q"].reshape(b, s, DIFF_HEADS, 2, DIFF_QK_DIM), dif_q_gain[l]), cos64, sin64)
        kc_x = apply_rope(rms_norm(px["dif_k"].reshape(b, s, DIFF_HEADS, 2, DIFF_QK_DIM), dif_k_gain[l]), cos64, sin64)
        vc_x = px["dif_v"].reshape(b, s, DIFF_HEADS, DIFF_V_DIM)
        kc_c = rms_norm(pc["dif_k"].reshape(b, lc, DIFF_HEADS, 2, DIFF_QK_DIM), dif_k_gain[l])
        vc_c = pc["dif_v"].reshape(b, lc, DIFF_HEADS, DIFF_V_DIM)
        kc_all = jnp.concatenate([kc_x, kc_c], axis=1)
        vc_all = jnp.concatenate([vc_x, vc_c], axis=1)
        oc_x = block_sweep(lambda qb: diff_attend(qb, kc_all, vc_all, lam), qc_x)
        yc_x = (rms_norm(oc_x, dif_out_gain[l]) * (1.0 - lam_init)).reshape(b, s, DIFF_WIDTH)

        y_x = jnp.concatenate([ya_x * jax.nn.silu(px["mla_gate"]),
                               yb_x * jax.nn.silu(px["swa_gate"]),
                               yc_x * jax.nn.silu(px["dif_gate"])], axis=-1) @ w_out[l]

        if need_ctx_out:
            qa_c = mla_queries(pc["mla_cq"], mla_q_norm[l], mla_w_uq[l], mla_q_gain[l])
            ya_c = attend(qa_c, ka_c, va_c, MLA_QK ** -0.5).reshape(b, lc, MLA_WIDTH)
            qb_c = rms_norm(pc["swa_q"].reshape(b, lc, SWA_KV_HEADS, SWA_GROUP, SWA_HEAD_DIM), swa_q_gain[l])
            yb_c = swa_context(qb_c, kb_c, vb_c, swa_sink[l])
            qc_c = rms_norm(pc["dif_q"].reshape(b, lc, DIFF_HEADS, 2, DIFF_QK_DIM), dif_q_gain[l])
            yc_c = (rms_norm(diff_attend(qc_c, kc_c, vc_c, lam), dif_out_gain[l]) * (1.0 - lam_init)).reshape(b, lc, DIFF_WIDTH)
            y_c = jnp.concatenate([ya_c * jax.nn.silu(pc["mla_gate"]),
                                   yb_c * jax.nn.silu(pc["swa_gate"]),
                                   yc_c * jax.nn.silu(pc["dif_gate"])], axis=-1) @ w_out[l]
            hc = hc + g_c * y_c

        hx = hx + g_x * y_x

    return hx
```

```python
import functools
import math

import numpy as np
import jax
import jax.numpy as jnp
from jax import lax
from jax.experimental import pallas as pl
from jax.experimental.pallas import tpu as pltpu

F32 = jnp.float32
BF16 = jnp.bfloat16

GRID_W = 64
ROPE_BASE = 10000.0
EPS = 1e-6
LANE = 128

MLA_HEADS = 6
MLA_Q_RANK = 512
MLA_KV_RANK = 256
MLA_NOPE = 128
MLA_ROPE = 64
MLA_V = 128
MLA_QK = MLA_NOPE + MLA_ROPE
MLA_QK_PAD = 2 * LANE
MLA_WIDTH = MLA_HEADS * MLA_V

SWA_Q_HEADS = 6
SWA_KV_HEADS = 2
SWA_GROUP = SWA_Q_HEADS // SWA_KV_HEADS
SWA_HEAD_DIM = 128
SWA_WINDOW = 128
SWA_WIDTH = SWA_Q_HEADS * SWA_HEAD_DIM

DIFF_HEADS = 4
DIFF_QK_DIM = 64
DIFF_V_DIM = 2 * DIFF_QK_DIM
DIFF_WIDTH = DIFF_HEADS * DIFF_V_DIM

MIX_WIDTH = MLA_WIDTH + SWA_WIDTH + DIFF_WIDTH

LOG2E = math.log2(math.e)
NEG_BIG = -1e30

C_CQ = 0
C_CKV = C_CQ + MLA_Q_RANK
C_KR = C_CKV + MLA_KV_RANK
C_SWA_Q = 1024
C_SWA_K = C_SWA_Q + SWA_WIDTH
C_SWA_V = C_SWA_K + SWA_KV_HEADS * SWA_HEAD_DIM
C_DIF_Q = C_SWA_V + SWA_KV_HEADS * SWA_HEAD_DIM
C_DIF_K = C_DIF_Q + DIFF_WIDTH
C_DIF_V = C_DIF_K + DIFF_WIDTH
C_QKV_END = 4096
C_GATE = C_QKV_END
P_COLS = C_GATE + MIX_WIDTH

VMEM_LIMIT = 56 * 1024 * 1024


def _cparams(sem):
    return pltpu.CompilerParams(dimension_semantics=sem, vmem_limit_bytes=VMEM_LIMIT)


def _mod_kernel(c_ref, w_ref, b_ref, o_ref):
    c = c_ref[...]
    a = c * (1.0 / (1.0 + jnp.exp(-c)))
    o_ref[0] = jnp.dot(a, w_ref[0], preferred_element_type=F32, precision=lax.Precision.HIGHEST) + b_ref[0]


def _modulation(cvec, w_ada, b_ada):
    depth, d, n3 = w_ada.shape
    rows = cvec.shape[0]
    tn = 1536
    return pl.pallas_call(
        _mod_kernel,
        grid=(depth, n3 // tn),
        in_specs=[
            pl.BlockSpec((rows, d), lambda l, n: (0, 0)),
            pl.BlockSpec((1, d, tn), lambda l, n: (l, 0, n)),
            pl.BlockSpec((1, 1, tn), lambda l, n: (l, 0, n)),
        ],
        out_specs=pl.BlockSpec((1, rows, tn), lambda l, n: (l, 0, n)),
        out_shape=jax.ShapeDtypeStruct((depth, rows, n3), F32),
        compiler_params=_cparams(("parallel", "parallel")),
        name="adaln_modulation",
    )(cvec, w_ada, b_ada.reshape(depth, 1, n3))


def _inproj_kernel(x_ref, sc_ref, sh_ref, g_ref, w_ref, o_ref, xn_ref):
    @pl.when(pl.program_id(2) == 0)
    def _():
        x = x_ref[0]
        r = lax.rsqrt(jnp.mean(x * x, axis=-1, keepdims=True) + EPS)
        y = (x * r) * g_ref[...]
        xn_ref[...] = (y * (1.0 + sc_ref[0]) + sh_ref[0]).astype(BF16)

    o_ref[0] = jnp.dot(xn_ref[...], w_ref[...], preferred_element_type=F32).astype(BF16)


def _inproj(h, sc, sh, norm_g, w, tm, tn):
    bx, tx, d = h.shape
    n = w.shape[1]
    return pl.pallas_call(
        _inproj_kernel,
        grid=(bx, tx // tm, n // tn),
        in_specs=[
            pl.BlockSpec((1, tm, d), lambda b, i, j: (b, i, 0)),
            pl.BlockSpec((1, 1, d), lambda b, i, j: (b, 0, 0)),
            pl.BlockSpec((1, 1, d), lambda b, i, j: (b, 0, 0)),
            pl.BlockSpec((1, d), lambda b, i, j: (0, 0)),
            pl.BlockSpec((d, tn), lambda b, i, j: (0, j)),
        ],
        out_specs=pl.BlockSpec((1, tm, tn), lambda b, i, j: (b, i, j)),
        out_shape=jax.ShapeDtypeStruct((bx, tx, n), BF16),
        scratch_shapes=[pltpu.VMEM((tm, d), BF16)],
        compiler_params=_cparams(("parallel", "parallel", "arbitrary")),
        name="norm_mod_inproj",
    )(h, sc, sh, norm_g, w)


def _rope64(x, cd, sneg, spos):
    return x * cd + pltpu.roll(x, 96, 1) * sneg + pltpu.roll(x, 32, 1) * spos


def _rope128(x, cs, ss):
    return x * cs + pltpu.roll(x, 64, 1) * ss


def _prep_kernel(*refs, rope):
    if rope:
        (p_ref, cd_ref, sneg_ref, spos_ref, cs_ref, ss_ref,
         qn_g_ref, wuq_ref, qgain_ref, kvn_g_ref, wkn_ref, wv_ref, kgn_ref, kgr_ref,
         sqg_ref, skg_ref, dqg_ref, dkg_ref, bd_ref,
         qa_ref, ka_ref, va_ref, qb_ref, kb_ref, qc_ref, kc_ref) = refs
        cd, sneg, spos = cd_ref[...], sneg_ref[...], spos_ref[...]
        cs, ss = cs_ref[...], ss_ref[...]
        rope64 = lambda t: _rope64(t, cd, sneg, spos)
        rope128 = lambda t: _rope128(t, cs, ss)
    else:
        (p_ref, qn_g_ref, wuq_ref, qgain_ref, kvn_g_ref, wkn_ref, wv_ref, kgn_ref, kgr_ref,
         sqg_ref, skg_ref, dqg_ref, dkg_ref, bd_ref,
         qa_ref, ka_ref, va_ref, qb_ref, kb_ref, qc_ref, kc_ref) = refs
        rope64 = rope128 = lambda t: t

    def cols(lo, n):
        return p_ref[0, :, lo:lo + n].astype(F32)

    def rms(t, width):
        return lax.rsqrt(jnp.sum(t * t, axis=-1, keepdims=True) * (1.0 / width) + EPS)

    cq = cols(C_CQ, MLA_Q_RANK)
    cqn = ((cq * rms(cq, MLA_Q_RANK)) * qn_g_ref[...]).astype(BF16)
    q = jnp.dot(cqn, wuq_ref[...], preferred_element_type=F32)
    qscale = MLA_QK ** -0.5 * LOG2E
    for h in range(MLA_HEADS):
        qh = q[:, h * MLA_QK_PAD:(h + 1) * MLA_QK_PAD]
        qh = (qh * rms(qh, MLA_QK)) * qgain_ref[...]
        qa_ref[0, :, h * MLA_QK_PAD:h * MLA_QK_PAD + LANE] = (qh[:, :LANE] * qscale).astype(BF16)
        qa_ref[0, :, h * MLA_QK_PAD + LANE:(h + 1) * MLA_QK_PAD] = (rope64(qh[:, LANE:]) * qscale).astype(BF16)

    ckv = cols(C_CKV, MLA_KV_RANK)
    ckvn = ((ckv * rms(ckv, MLA_KV_RANK)) * kvn_g_ref[...]).astype(BF16)
    kn = jnp.dot(ckvn, wkn_ref[...], preferred_element_type=F32)
    va_ref[0] = jnp.dot(ckvn, wv_ref[...], preferred_element_type=F32).astype(BF16)
    krp = cols(C_KR, LANE)
    kr_ssq = jnp.sum(krp * krp, axis=-1, keepdims=True)
    kr_rot = rope64(krp * kgr_ref[...])
    for h in range(MLA_HEADS):
        knh = kn[:, h * LANE:(h + 1) * LANE]
        r = lax.rsqrt((jnp.sum(knh * knh, axis=-1, keepdims=True) + kr_ssq) * (1.0 / MLA_QK) + EPS)
        ka_ref[0, :, h * MLA_QK_PAD:h * MLA_QK_PAD + LANE] = ((knh * r) * kgn_ref[...]).astype(BF16)
        ka_ref[0, :, h * MLA_QK_PAD + LANE:(h + 1) * MLA_QK_PAD] = (kr_rot * r).astype(BF16)

    bscale = SWA_HEAD_DIM ** -0.5 * LOG2E
    for h in range(SWA_Q_HEADS):
        t = cols(C_SWA_Q + h * LANE, LANE)
        t = rope128((t * rms(t, SWA_HEAD_DIM)) * sqg_ref[...])
        qb_ref[0, :, h * LANE:(h + 1) * LANE] = (t * bscale).astype(BF16)
    for h in range(SWA_KV_HEADS):
        t = cols(C_SWA_K + h * LANE, LANE)
        kb_ref[0, :, h * LANE:(h + 1) * LANE] = rope128((t * rms(t, SWA_HEAD_DIM)) * skg_ref[...]).astype(BF16)

    def seg_norm(t, gain):
        t2 = t * t
        hi = t2.astype(BF16)
        lo = (t2 - hi.astype(F32)).astype(BF16)
        ssq = (jnp.dot(hi, bd_ref[...], preferred_element_type=F32)
               + jnp.dot(lo, bd_ref[...], preferred_element_type=F32))
        return (t * lax.rsqrt(ssq * (1.0 / DIFF_QK_DIM) + EPS)) * gain

    cscale = DIFF_QK_DIM ** -0.5 * LOG2E
    tq = seg_norm(cols(C_DIF_Q, DIFF_WIDTH), dqg_ref[...])
    tk = seg_norm(cols(C_DIF_K, DIFF_WIDTH), dkg_ref[...])
    for j in range(DIFF_WIDTH // LANE):
        sl = slice(j * LANE, (j + 1) * LANE)
        qc_ref[0, :, sl] = (rope64(tq[:, sl]) * cscale).astype(BF16)
        kc_ref[0, :, sl] = rope64(tk[:, sl]).astype(BF16)


def _prep(p, tables, wl, tm):
    bx, tx, _ = p.shape
    rope = tables is not None
    row = lambda a: a.reshape(1, -1)
    full = lambda a: pl.BlockSpec(a.shape, lambda b, i: (0,) * a.ndim)
    consts = [row(wl["mla_q_norm"]), wl["w_uq"], row(wl["q_gain"]), row(wl["mla_kv_norm"]), wl["w_kn"], wl["w_v"],
              row(wl["k_gain_nope"]), row(wl["k_gain_rope"]), row(wl["swa_q_gain"]), row(wl["swa_k_gain"]),
              row(wl["dif_q_gain"]), row(wl["dif_k_gain"]), wl["bd"]]
    ins = [p]
    in_specs = [pl.BlockSpec((1, tm, C_QKV_END), lambda b, i: (b, i, 0))]
    if rope:
        ins += list(tables)
        in_specs += [pl.BlockSpec((tm, LANE), lambda b, i: (i, 0))] * len(tables)
    ins += consts
    in_specs += [full(a) for a in consts]
    widths = [MLA_HEADS * MLA_QK_PAD, MLA_HEADS * MLA_QK_PAD, MLA_WIDTH, SWA_WIDTH,
              SWA_KV_HEADS * SWA_HEAD_DIM, DIFF_WIDTH, DIFF_WIDTH]
    return pl.pallas_call(
        functools.partial(_prep_kernel, rope=rope),
        grid=(bx, tx // tm),
        in_specs=in_specs,
        out_specs=[pl.BlockSpec((1, tm, w), lambda b, i: (b, i, 0)) for w in widths],
        out_shape=[jax.ShapeDtypeStruct((bx, tx, w), BF16) for w in widths],
        compiler_params=_cparams(("parallel", "parallel")),
        name="head_prep_rope" if rope else "head_prep",
    )(*ins)


def _fold_lanes(t, op):
    parts = [t[:, c * LANE:(c + 1) * LANE] for c in range(t.shape[1] // LANE)]
    return functools.reduce(op, parts)


def _key_chunks(srcs, tk):
    out = []
    for k_ref, v_ref in srcs:
        n = k_ref.shape[0]
        for c0 in range(0, n, tk):
            out.append((k_ref, v_ref, c0, min(tk, n - c0)))
    return out


def _dense_attention(q, srcs, s_scr, tk, sink=None):
    chunks = _key_chunks(srcs, tk)
    mpart = None
    off = 0
    for k_ref, _, c0, n in chunks:
        s = lax.dot_general(q, k_ref[c0:c0 + n, :], (((1,), (1,)), ((), ())), preferred_element_type=F32)
        s_scr[:, off:off + n] = s
        f = _fold_lanes(s, jnp.maximum)
        mpart = f if mpart is None else jnp.maximum(mpart, f)
        off += n
    m = jnp.max(mpart, axis=-1, keepdims=True)
    if sink is not None:
        m = jnp.maximum(m, sink)
    lpart = None
    acc = None
    off = 0
    for _, v_ref, c0, n in chunks:
        p = jnp.exp2(s_scr[:, off:off + n] - m)
        f = _fold_lanes(p, jnp.add)
        lpart = f if lpart is None else lpart + f
        pv = jnp.dot(p.astype(BF16), v_ref[c0:c0 + n, :], preferred_element_type=F32)
        acc = pv if acc is None else acc + pv
        off += n
    l = jnp.sum(lpart, axis=-1, keepdims=True)
    if sink is not None:
        l = l + jnp.exp2(sink - m)
    return acc * (1.0 / l)


class _View:
    def __init__(self, ref, lead):
        self.ref, self.lead = ref, lead
        self.shape = ref.shape[len(lead):]

    def __getitem__(self, idx):
        return self.ref[self.lead + idx]


def _mla_kernel(*refs, n_src, tk):
    q_ref = refs[0]
    kv = refs[1:1 + 2 * n_src]
    o_ref = refs[1 + 2 * n_src]
    s_scr = refs[2 + 2 * n_src]
    srcs = [(_View(kv[2 * j], (0,)), _View(kv[2 * j + 1], (0,))) for j in range(n_src)]
    o_ref[0] = _dense_attention(q_ref[0], srcs, s_scr, tk).astype(BF16)


def _mla_attention(q, kv_list, tq, tk):
    b, t_q, _ = q.shape
    ins = [q]
    in_specs = [pl.BlockSpec((1, tq, MLA_QK_PAD), lambda bi, h, i: (bi, i, h))]
    total = 0
    for k, v in kv_list:
        t = k.shape[1]
        total += t
        ins += [k, v]
        in_specs += [pl.BlockSpec((1, t, MLA_QK_PAD), lambda bi, h, i: (bi, 0, h)),
                     pl.BlockSpec((1, t, MLA_V), lambda bi, h, i: (bi, 0, h))]
    return pl.pallas_call(
        functools.partial(_mla_kernel, n_src=len(kv_list), tk=tk),
        grid=(b, MLA_HEADS, t_q // tq),
        in_specs=in_specs,
        out_specs=pl.BlockSpec((1, tq, MLA_V), lambda bi, h, i: (bi, i, h)),
        out_shape=jax.ShapeDtypeStruct((b, t_q, MLA_WIDTH), BF16),
        scratch_shapes=[pltpu.VMEM((tq, total), F32)],
        compiler_params=_cparams(("parallel", "parallel", "arbitrary")),
        name="mla_attention",
    )(*ins)


def _diff_kernel(*refs, n_src, tk, lam_init):
    q_ref = refs[0]
    kv = refs[1:1 + 2 * n_src]
    lq1_ref, lk1_ref, lq2_ref, lk2_ref, og_ref, o_ref, s_scr = refs[1 + 2 * n_src:]
    srcs = [(_View(kv[2 * j], (0,)), _View(kv[2 * j + 1], (0,))) for j in range(n_src)]
    q = q_ref[0]
    tq = q.shape[0]
    first = lax.broadcasted_iota(jnp.int32, q.shape, 1) < DIFF_QK_DIM
    zero = jnp.zeros_like(q)
    qq = jnp.concatenate([jnp.where(first, q, zero), jnp.where(first, zero, q)], axis=0)
    o = _dense_attention(qq, srcs, s_scr, tk)
    lam = (jnp.exp(jnp.sum(lq1_ref[...] * lk1_ref[...], axis=-1, keepdims=True))
           - jnp.exp(jnp.sum(lq2_ref[...] * lk2_ref[...], axis=-1, keepdims=True)) + lam_init)
    d = o[:tq] - lam * o[tq:]
    r = lax.rsqrt(jnp.mean(d * d, axis=-1, keepdims=True) + EPS)
    o_ref[0] = (((d * r) * og_ref[...]) * (1.0 - lam_init)).astype(BF16)


def _diff_attention(q, kv_list, v_col0, lams, out_gain, lam_init, tq, tk):
    b, t_q, _ = q.shape
    ins = [q]
    in_specs = [pl.BlockSpec((1, tq, LANE), lambda bi, h, i: (bi, i, h))]
    total = 0
    for k, pv in kv_list:
        t = k.shape[1]
        total += t
        ins += [k, pv]
        in_specs += [pl.BlockSpec((1, t, LANE), lambda bi, h, i: (bi, 0, h)),
                     pl.BlockSpec((1, t, LANE), lambda bi, h, i: (bi, 0, v_col0 + h))]
    small = [a.reshape(1, -1) for a in lams] + [out_gain.reshape(1, -1)]
    ins += small
    in_specs += [pl.BlockSpec(a.shape, lambda bi, h, i: (0, 0)) for a in small]
    return pl.pallas_call(
        functools.partial(_diff_kernel, n_src=len(kv_list), tk=tk, lam_init=lam_init),
        grid=(b, DIFF_HEADS, t_q // tq),
        in_specs=in_specs,
        out_specs=pl.BlockSpec((1, tq, DIFF_V_DIM), lambda bi, h, i: (bi, i, h)),
        out_shape=jax.ShapeDtypeStruct((b, t_q, DIFF_WIDTH), BF16),
        scratch_shapes=[pltpu.VMEM((2 * tq, total), F32)],
        compiler_params=_cparams(("parallel", "parallel", "arbitrary")),
        name="diff_attention",
    )(*ins)


def _stack_heads(q_ref):
    return jnp.concatenate([q_ref[0, :, g * LANE:(g + 1) * LANE] for g in range(SWA_GROUP)], axis=0)


def _sink_rows(sink_ref, h, tq):
    return jnp.concatenate(
        [jnp.full((tq, 1), sink_ref[h * SWA_GROUP + g] * LOG2E, F32) for g in range(SWA_GROUP)], axis=0)


def _swa_ctx_kernel(sink_ref, q_ref, k_ref, v_ref, o_ref, s_scr):
    tq = q_ref.shape[1]
    o = _dense_attention(_stack_heads(q_ref), [(_View(k_ref, (0,)), _View(v_ref, (0,)))], s_scr, k_ref.shape[1],
                         sink=_sink_rows(sink_ref, pl.program_id(1), tq))
    for g in range(SWA_GROUP):
        o_ref[0, :, g * LANE:(g + 1) * LANE] = o[g * tq:(g + 1) * tq].astype(BF16)


def _swa_window_kernel(sink_ref, q_ref, k_ref, v_ref, kc_ref, vc_ref, o_ref):
    h, i = pl.program_id(1), pl.program_id(2)
    tq = q_ref.shape[1]
    s_len = k_ref.shape[1]
    win = tq + 2 * SWA_WINDOW
    start = pl.multiple_of(jnp.clip(i * tq - SWA_WINDOW, 0, s_len - win), LANE)
    qq = _stack_heads(q_ref)
    kw = k_ref[0, pl.ds(start, win), :]
    vw = v_ref[0, pl.ds(start, win), :]
    dn = (((1,), (1,)), ((), ()))
    s_w = lax.dot_general(qq, kw, dn, preferred_element_type=F32)
    s_c = lax.dot_general(qq, kc_ref[0], dn, preferred_element_type=F32)
    rel = (lax.broadcasted_iota(jnp.int32, (tq, win), 1) - lax.broadcasted_iota(jnp.int32, (tq, win), 0)
           + (start - i * tq))
    bias = jnp.where(jnp.abs(rel) <= SWA_WINDOW, 0.0, NEG_BIG).astype(F32)
    s_w = s_w + jnp.concatenate([bias] * SWA_GROUP, axis=0)
    sink = _sink_rows(sink_ref, h, tq)
    m = jnp.maximum(jnp.maximum(jnp.max(s_w, axis=-1, keepdims=True), jnp.max(s_c, axis=-1, keepdims=True)), sink)
    p_w = jnp.exp2(s_w - m)
    p_c = jnp.exp2(s_c - m)
    l = jnp.sum(p_w, axis=-1, keepdims=True) + jnp.sum(p_c, axis=-1, keepdims=True) + jnp.exp2(sink - m)
    o = (jnp.dot(p_w.astype(BF16), vw, preferred_element_type=F32)
         + jnp.dot(p_c.astype(BF16), vc_ref[0], preferred_element_type=F32)) * (1.0 / l)
    for g in range(SWA_GROUP):
        o_ref[0, :, g * LANE:(g + 1) * LANE] = o[g * tq:(g + 1) * tq].astype(BF16)


def _swa_attention(q, k, pv, kc, pvc, sink, tq):
    b, s_len, _ = q.shape
    lc = kc.shape[1]
    vblk = C_SWA_V // LANE
    qspec = pl.BlockSpec((1, tq, SWA_GROUP * LANE), lambda bi, h, i: (bi, i, h))
    return pl.pallas_call(
        _swa_window_kernel,
        grid=(b, SWA_KV_HEADS, s_len // tq),
        in_specs=[
            pl.BlockSpec(memory_space=pltpu.SMEM),
            qspec,
            pl.BlockSpec((1, s_len, LANE), lambda bi, h, i: (bi, 0, h)),
            pl.BlockSpec((1, s_len, LANE), lambda bi, h, i: (bi, 0, vblk + h)),
            pl.BlockSpec((1, lc, LANE), lambda bi, h, i: (bi, 0, h)),
            pl.BlockSpec((1, lc, LANE), lambda bi, h, i: (bi, 0, vblk + h)),
        ],
        out_specs=qspec,
        out_shape=jax.ShapeDtypeStruct((b, s_len, SWA_WIDTH), BF16),
        compiler_params=_cparams(("parallel", "parallel", "arbitrary")),
        name="swa_window_attention",
    )(sink, q, k, pv, kc, pvc)


def _swa_ctx_attention(q, k, pv, sink):
    b, lc, _ = q.shape
    vblk = C_SWA_V // LANE
    qspec = pl.BlockSpec((1, lc, SWA_GROUP * LANE), lambda bi, h: (bi, 0, h))
    return pl.pallas_call(
        _swa_ctx_kernel,
        grid=(b, SWA_KV_HEADS),
        in_specs=[
            pl.BlockSpec(memory_space=pltpu.SMEM),
            qspec,
            pl.BlockSpec((1, lc, LANE), lambda bi, h: (bi, 0, h)),
            pl.BlockSpec((1, lc, LANE), lambda bi, h: (bi, 0, vblk + h)),
        ],
        out_specs=qspec,
        out_shape=jax.ShapeDtypeStruct((b, lc, SWA_WIDTH), BF16),
        scratch_shapes=[pltpu.VMEM((SWA_GROUP * lc, lc), F32)],
        compiler_params=_cparams(("parallel", "parallel")),
        name="swa_ctx_attention",
    )(sink, q, k, pv)


def _outproj_kernel(ya_ref, yb_ref, yc_ref, gate_ref, h_ref, g_ref, w_ref, o_ref):
    y = jnp.concatenate([ya_ref[0], yb_ref[0], yc_ref[0]], axis=-1).astype(F32)
    gt = gate_ref[0].astype(F32)
    a = (y * (gt * (1.0 / (1.0 + jnp.exp(-gt))))).astype(BF16)
    o_ref[0] = h_ref[0] + g_ref[0] * jnp.dot(a, w_ref[...], preferred_element_type=F32)


def _outproj(ya, yb, yc, p, h, g, w, tm):
    bx, tx, d = h.shape
    tok = lambda wd: pl.BlockSpec((1, tm, wd), lambda b, i: (b, i, 0))
    return pl.pallas_call(
        _outproj_kernel,
        grid=(bx, tx // tm),
        in_specs=[
            tok(MLA_WIDTH), tok(SWA_WIDTH), tok(DIFF_WIDTH),
            pl.BlockSpec((1, tm, MIX_WIDTH), lambda b, i: (b, i, C_GATE // MIX_WIDTH)),
            tok(d),
            pl.BlockSpec((1, 1, d), lambda b, i: (b, 0, 0)),
            pl.BlockSpec(w.shape, lambda b, i: (0, 0)),
        ],
        out_specs=tok(d),
        out_shape=jax.ShapeDtypeStruct((bx, tx, d), F32),
        compiler_params=_cparams(("parallel", "parallel")),
        name="gated_outproj_residual",
    )(ya, yb, yc, p, h, g, w)


def _layer_weights(l, w_in, mla_w_uq, mla_w_ukv, mla_q_gain, mla_k_gain, w_out):
    d = w_in.shape[1]
    wi = w_in[l]
    o = np.cumsum([0, MLA_Q_RANK, MLA_KV_RANK, MLA_ROPE, MLA_WIDTH, SWA_WIDTH, SWA_KV_HEADS * SWA_HEAD_DIM,
                   SWA_KV_HEADS * SWA_HEAD_DIM, SWA_WIDTH, DIFF_WIDTH, DIFF_WIDTH, DIFF_WIDTH, DIFF_WIDTH]).tolist()
    seg = lambda j: wi[:, o[j]:o[j + 1]]
    z = lambda n: jnp.zeros((d, n), wi.dtype)
    w_p = jnp.concatenate([seg(0), seg(1), seg(2), z(C_SWA_Q - C_KR - MLA_ROPE),
                           seg(4), seg(5), seg(6), seg(8), seg(9), seg(10), z(C_QKV_END - C_DIF_V - DIFF_WIDTH),
                           seg(3), seg(7), seg(11)], axis=1).astype(BF16)
    pad = MLA_QK_PAD - MLA_QK
    w_uq = jnp.pad(mla_w_uq[l].reshape(MLA_Q_RANK, MLA_HEADS, MLA_QK), ((0, 0), (0, 0), (0, pad)))
    w_ukv = mla_w_ukv[l].reshape(MLA_KV_RANK, MLA_HEADS, MLA_NOPE + MLA_V)
    return {
        "w_p": w_p,
        "w_uq": w_uq.reshape(MLA_Q_RANK, MLA_HEADS * MLA_QK_PAD).astype(BF16),
        "w_kn": w_ukv[:, :, :MLA_NOPE].reshape(MLA_KV_RANK, MLA_HEADS * MLA_NOPE).astype(BF16),
        "w_v": w_ukv[:, :, MLA_NOPE:].reshape(MLA_KV_RANK, MLA_WIDTH).astype(BF16),
        "q_gain": jnp.pad(mla_q_gain[l], (0, pad)),
        "k_gain_nope": mla_k_gain[l][:MLA_NOPE],
        "k_gain_rope": jnp.pad(mla_k_gain[l][MLA_NOPE:], (0, LANE - MLA_ROPE)),
        "w_out": w_out[l].astype(BF16),
    }


def _rope_tables(n_tokens):
    t = np.arange(n_tokens)
    row, col = (t // GRID_W).astype(np.float64), (t % GRID_W).astype(np.float64)

    def cos_sin(rot_dim):
        n_freq = rot_dim // 4
        inv = np.power(ROPE_BASE, -np.arange(n_freq, dtype=np.float32) / n_freq).astype(np.float32)
        ang = np.concatenate([row[:, None].astype(np.float32) * inv, col[:, None].astype(np.float32) * inv], axis=-1)
        return np.cos(ang).astype(np.float32), np.sin(ang).astype(np.float32)

    c64, s64 = cos_sin(MLA_ROPE)
    c128, s128 = cos_sin(SWA_HEAD_DIM)
    zeros = np.zeros_like(s64)
    cd = np.concatenate([c64] * 4, axis=-1)
    sneg = np.concatenate([-s64, zeros, -s64, zeros], axis=-1)
    spos = np.concatenate([zeros, s64, zeros, s64], axis=-1)
    cs = np.concatenate([c128, c128], axis=-1)
    ss = np.concatenate([-s128, s128], axis=-1)
    return tuple(jnp.asarray(a, F32) for a in (cd, sneg, spos, cs, ss))


def _block_diag_ones(n, seg):
    idx = np.arange(n) // seg
    return jnp.asarray(idx[:, None] == idx[None, :], BF16)


def _pick(n, prefs):
    for t in prefs:
        if n % t == 0:
            return t
    return n


def kernel(x, c, ctx, c_ctx, norm_g, w_ada, b_ada, w_in, mla_q_norm, mla_w_uq, mla_kv_norm, mla_w_ukv,
           mla_q_gain, mla_k_gain, swa_q_gain, swa_k_gain, swa_sink, dif_q_gain, dif_k_gain,
           dif_lq1, dif_lk1, dif_lq2, dif_lk2, dif_out_gain, w_out):
    b, s, d = x.shape
    lc = ctx.shape[1]
    depth = w_in.shape[0]
    assert s % (2 * LANE) == 0 and s >= 4 * LANE and lc % LANE == 0 and d == MIX_WIDTH

    tables = _rope_tables(s)
    bd = _block_diag_ones(DIFF_WIDTH, DIFF_QK_DIM)

    n_rows = -(-(b + 1) // 8) * 8
    cvec = jnp.concatenate([c, c_ctx[None, :], jnp.zeros((n_rows - b - 1, d), F32)], axis=0)
    mod = _modulation(cvec, w_ada, b_ada)

    hx = x
    hc = ctx.reshape(1, b * lc, d)
    tm_x = _pick(s, (1024, 512, 256))
    tm_c = _pick(b * lc, (1024, 512, 256))
    tq_a = _pick(s, (512, 256))
    tq_b = _pick(s, (256,))
    tq_c = _pick(s, (256,))
    tk = 512

    for l in range(depth):
        need_ctx_out = l < depth - 1
        wl = _layer_weights(l, w_in, mla_w_uq, mla_w_ukv, mla_q_gain, mla_k_gain, w_out)
        wl.update(mla_q_norm=mla_q_norm[l], mla_kv_norm=mla_kv_norm[l], swa_q_gain=swa_q_gain[l],
                  swa_k_gain=swa_k_gain[l], dif_q_gain=jnp.tile(dif_q_gain[l], DIFF_WIDTH // DIFF_QK_DIM),
                  dif_k_gain=jnp.tile(dif_k_gain[l], DIFF_WIDTH // DIFF_QK_DIM), bd=bd)
        m = mod[l]
        sh_x, sc_x, g_x = (m[:b, j * d:(j + 1) * d].reshape(b, 1, d) for j in range(3))
        sh_c, sc_c, g_c = (m[b:b + 1, j * d:(j + 1) * d].reshape(1, 1, d) for j in range(3))
        ng = norm_g[l].reshape(1, d)
        lam_init = 0.8 - 0.6 * math.exp(-0.3 * l)
        lams = (dif_lq1[l], dif_lk1[l], dif_lq2[l], dif_lk2[l])
        sink = swa_sink[l]

        px = _inproj(hx, sc_x, sh_x, ng, wl["w_p"], tm_x, 1024)
        pc = _inproj(hc, sc_c, sh_c, ng, wl["w_p"], tm_c, 1024)
        qa_x, ka_x, va_x, qb_x, kb_x, qc_x, kc_x = _prep(px, tables, wl, _pick(s, (512, 256)))
        qa_c, ka_c, va_c, qb_c, kb_c, qc_c, kc_c = (
            t.reshape(b, lc, -1) for t in _prep(pc, None, wl, _pick(b * lc, (512, 256))))
        pcb = pc.reshape(b, lc, P_COLS)
        vblk_c = C_DIF_V // LANE

        ya_x = _mla_attention(qa_x, [(ka_x, va_x), (ka_c, va_c)], tq_a, tk)
        yb_x = _swa_attention(qb_x, kb_x, px, kb_c, pcb, sink, tq_b)
        yc_x = _diff_attention(qc_x, [(kc_x, px), (kc_c, pcb)], vblk_c, lams, dif_out_gain[l], lam_init, tq_c, tk)

        if need_ctx_out:
            ya_c = _mla_attention(qa_c, [(ka_c, va_c)], lc, tk)
            yb_c = _swa_ctx_attention(qb_c, kb_c, pcb, sink)
            yc_c = _diff_attention(qc_c, [(kc_c, pcb)], vblk_c, lams, dif_out_gain[l], lam_init, lc, tk)
            flat = lambda t: t.reshape(1, b * lc, -1)
            hc = _outproj(flat(ya_c), flat(yb_c), flat(yc_c), pc, hc, g_c, wl["w_out"], _pick(b * lc, (512, 256)))

        hx = _outproj(ya_x, yb_x, yc_x, px, hx, g_x, wl["w_out"], _pick(s, (512, 256)))

    return hx
```

```python
import functools
import math

import numpy as np
import jax
import jax.numpy as jnp
from jax import lax
from jax.experimental import pallas as pl
from jax.experimental.pallas import tpu as pltpu

F32 = jnp.float32
BF16 = jnp.bfloat16

GRID_W = 64
ROPE_BASE = 10000.0
EPS = 1e-6
LANE = 128

MLA_HEADS = 6
MLA_Q_RANK = 512
MLA_KV_RANK = 256
MLA_NOPE = 128
MLA_ROPE = 64
MLA_V = 128
MLA_QK = MLA_NOPE + MLA_ROPE
MLA_QK_PAD = 2 * LANE
MLA_WIDTH = MLA_HEADS * MLA_V

SWA_Q_HEADS = 6
SWA_KV_HEADS = 2
SWA_GROUP = SWA_Q_HEADS // SWA_KV_HEADS
SWA_HEAD_DIM = 128
SWA_WINDOW = 128
SWA_WIDTH = SWA_Q_HEADS * SWA_HEAD_DIM

DIFF_HEADS = 4
DIFF_QK_DIM = 64
DIFF_V_DIM = 2 * DIFF_QK_DIM
DIFF_WIDTH = DIFF_HEADS * DIFF_V_DIM

MIX_WIDTH = MLA_WIDTH + SWA_WIDTH + DIFF_WIDTH

LOG2E = math.log2(math.e)
NEG_BIG = -1e30

C_CQ = 0
C_CKV = C_CQ + MLA_Q_RANK
C_KR = C_CKV + MLA_KV_RANK
C_SWA_Q = 1024
C_SWA_K = C_SWA_Q + SWA_WIDTH
C_SWA_V = C_SWA_K + SWA_KV_HEADS * SWA_HEAD_DIM
C_DIF_Q = C_SWA_V + SWA_KV_HEADS * SWA_HEAD_DIM
C_DIF_K = C_DIF_Q + DIFF_WIDTH
C_DIF_V = C_DIF_K + DIFF_WIDTH
C_QKV_END = 4096
C_GATE = C_QKV_END
P_COLS = C_GATE + MIX_WIDTH

VMEM_LIMIT = 56 * 1024 * 1024


def _cparams(sem):
    return pltpu.CompilerParams(dimension_semantics=sem, vmem_limit_bytes=VMEM_LIMIT)


def _mod_kernel(c_ref, w_ref, b_ref, o_ref):
    c = c_ref[...]
    a = c * (1.0 / (1.0 + jnp.exp(-c)))
    o_ref[0] = jnp.dot(a, w_ref[0], preferred_element_type=F32, precision=lax.Precision.HIGHEST) + b_ref[0]


def _modulation(cvec, w_ada, b_ada):
    depth, d, n3 = w_ada.shape
    rows = cvec.shape[0]
    tn = 1536
    return pl.pallas_call(
        _mod_kernel,
        grid=(depth, n3 // tn),
        in_specs=[
            pl.BlockSpec((rows, d), lambda l, n: (0, 0)),
            pl.BlockSpec((1, d, tn), lambda l, n: (l, 0, n)),
            pl.BlockSpec((1, 1, tn), lambda l, n: (l, 0, n)),
        ],
        out_specs=pl.BlockSpec((1, rows, tn), lambda l, n: (l, 0, n)),
        out_shape=jax.ShapeDtypeStruct((depth, rows, n3), F32),
        compiler_params=_cparams(("parallel", "parallel")),
        name="adaln_modulation",
    )(cvec, w_ada, b_ada.reshape(depth, 1, n3))


def _inproj_kernel(x_ref, sc_ref, sh_ref, g_ref, w_ref, o_ref, xn_ref):
    @pl.when(pl.program_id(2) == 0)
    def _():
        x = x_ref[0]
        r = lax.rsqrt(jnp.mean(x * x, axis=-1, keepdims=True) + EPS)
        y = (x * r) * g_ref[...]
        xn_ref[...] = (y * (1.0 + sc_ref[0]) + sh_ref[0]).astype(BF16)

    o_ref[0] = jnp.dot(xn_ref[...], w_ref[...], preferred_element_type=F32).astype(BF16)


def _inproj(h, sc, sh, norm_g, w, tm, tn):
    bx, tx, d = h.shape
    n = w.shape[1]
    return pl.pallas_call(
        _inproj_kernel,
        grid=(bx, tx // tm, n // tn),
        in_specs=[
            pl.BlockSpec((1, tm, d), lambda b, i, j: (b, i, 0)),
            pl.BlockSpec((1, 1, d), lambda b, i, j: (b, 0, 0)),
            pl.BlockSpec((1, 1, d), lambda b, i, j: (b, 0, 0)),
            pl.BlockSpec((1, d), lambda b, i, j: (0, 0)),
            pl.BlockSpec((d, tn), lambda b, i, j: (0, j)),
        ],
        out_specs=pl.BlockSpec((1, tm, tn), lambda b, i, j: (b, i, j)),
        out_shape=jax.ShapeDtypeStruct((bx, tx, n), BF16),
        scratch_shapes=[pltpu.VMEM((tm, d), BF16)],
        compiler_params=_cparams(("parallel", "parallel", "arbitrary")),
        name="norm_mod_inproj",
    )(h, sc, sh, norm_g, w)


def _rope64(x, cd, sneg, spos):
    return x * cd + pltpu.roll(x, 96, 1) * sneg + pltpu.roll(x, 32, 1) * spos


def _rope128(x, cs, ss):
    return x * cs + pltpu.roll(x, 64, 1) * ss


def _prep_kernel(*refs, rope):
    if rope:
        (p_ref, cd_ref, sneg_ref, spos_ref, cs_ref, ss_ref,
         qn_g_ref, wuq_ref, qgain_ref, kvn_g_ref, wkn_ref, wv_ref, kgn_ref, kgr_ref,
         sqg_ref, skg_ref, dqg_ref, dkg_ref, bd_ref,
         qa_ref, ka_ref, va_ref, qb_ref, kb_ref, qc_ref, kc_ref) = refs
        cd, sneg, spos = cd_ref[...], sneg_ref[...], spos_ref[...]
        cs, ss = cs_ref[...], ss_ref[...]
        rope64 = lambda t: _rope64(t, cd, sneg, spos)
        rope128 = lambda t: _rope128(t, cs, ss)
    else:
        (p_ref, qn_g_ref, wuq_ref, qgain_ref, kvn_g_ref, wkn_ref, wv_ref, kgn_ref, kgr_ref,
         sqg_ref, skg_ref, dqg_ref, dkg_ref, bd_ref,
         qa_ref, ka_ref, va_ref, qb_ref, kb_ref, qc_ref, kc_ref) = refs
        rope64 = rope128 = lambda t: t

    def cols(lo, n):
        return p_ref[0, :, lo:lo + n].astype(F32)

    def rms(t, width):
        return lax.rsqrt(jnp.sum(t * t, axis=-1, keepdims=True) * (1.0 / width) + EPS)

    cq = cols(C_CQ, MLA_Q_RANK)
    cqn = ((cq * rms(cq, MLA_Q_RANK)) * qn_g_ref[...]).astype(BF16)
    q = jnp.dot(cqn, wuq_ref[...], preferred_element_type=F32)
    qscale = MLA_QK ** -0.5 * LOG2E
    for h in range(MLA_HEADS):
        qh = q[:, h * MLA_QK_PAD:(h + 1) * MLA_QK_PAD]
        qh = (qh * rms(qh, MLA_QK)) * qgain_ref[...]
        qa_ref[0, :, h * MLA_QK_PAD:h * MLA_QK_PAD + LANE] = (qh[:, :LANE] * qscale).astype(BF16)
        qa_ref[0, :, h * MLA_QK_PAD + LANE:(h + 1) * MLA_QK_PAD] = (rope64(qh[:, LANE:]) * qscale).astype(BF16)

    ckv = cols(C_CKV, MLA_KV_RANK)
    ckvn = ((ckv * rms(ckv, MLA_KV_RANK)) * kvn_g_ref[...]).astype(BF16)
    kn = jnp.dot(ckvn, wkn_ref[...], preferred_element_type=F32)
    va_ref[0] = jnp.dot(ckvn, wv_ref[...], preferred_element_type=F32).astype(BF16)
    krp = cols(C_KR, LANE)
    kr_ssq = jnp.sum(krp * krp, axis=-1, keepdims=True)
    kr_rot = rope64(krp * kgr_ref[...])
    for h in range(MLA_HEADS):
        knh = kn[:, h * LANE:(h + 1) * LANE]
        r = lax.rsqrt((jnp.sum(knh * knh, axis=-1, keepdims=True) + kr_ssq) * (1.0 / MLA_QK) + EPS)
        ka_ref[0, :, h * MLA_QK_PAD:h * MLA_QK_PAD + LANE] = ((knh * r) * kgn_ref[...]).astype(BF16)
        ka_ref[0, :, h * MLA_QK_PAD + LANE:(h + 1) * MLA_QK_PAD] = (kr_rot * r).astype(BF16)

    bscale = SWA_HEAD_DIM ** -0.5 * LOG2E
    for h in range(SWA_Q_HEADS):
        t = cols(C_SWA_Q + h * LANE, LANE)
        t = rope128((t * rms(t, SWA_HEAD_DIM)) * sqg_ref[...])
        qb_ref[0, :, h * LANE:(h + 1) * LANE] = (t * bscale).astype(BF16)
    for h in range(SWA_KV_HEADS):
        t = cols(C_SWA_K + h * LANE, LANE)
        kb_ref[0, :, h * LANE:(h + 1) * LANE] = rope128((t * rms(t, SWA_HEAD_DIM)) * skg_ref[...]).astype(BF16)

    def seg_norm(t, gain):
        t2 = t * t
        hi = t2.astype(BF16)
        lo = (t2 - hi.astype(F32)).astype(BF16)
        ssq = (jnp.dot(hi, bd_ref[...], preferred_element_type=F32)
               + jnp.dot(lo, bd_ref[...], preferred_element_type=F32))
        return (t * lax.rsqrt(ssq * (1.0 / DIFF_QK_DIM) + EPS)) * gain

    cscale = DIFF_QK_DIM ** -0.5 * LOG2E
    tq = seg_norm(cols(C_DIF_Q, DIFF_WIDTH), dqg_ref[...])
    tk = seg_norm(cols(C_DIF_K, DIFF_WIDTH), dkg_ref[...])
    for j in range(DIFF_WIDTH // LANE):
        sl = slice(j * LANE, (j + 1) * LANE)
        qc_ref[0, :, sl] = (rope64(tq[:, sl]) * cscale).astype(BF16)
        kc_ref[0, :, sl] = rope64(tk[:, sl]).astype(BF16)


def _prep(p, tables, wl, tm):
    bx, tx, _ = p.shape
    rope = tables is not None
    row = lambda a: a.reshape(1, -1)
    full = lambda a: pl.BlockSpec(a.shape, lambda b, i: (0,) * a.ndim)
    consts = [row(wl["mla_q_norm"]), wl["w_uq"], row(wl["q_gain"]), row(wl["mla_kv_norm"]), wl["w_kn"], wl["w_v"],
              row(wl["k_gain_nope"]), row(wl["k_gain_rope"]), row(wl["swa_q_gain"]), row(wl["swa_k_gain"]),
              row(wl["dif_q_gain"]), row(wl["dif_k_gain"]), wl["bd"]]
    ins = [p]
    in_specs = [pl.BlockSpec((1, tm, C_QKV_END), lambda b, i: (b, i, 0))]
    if rope:
        ins += list(tables)
        in_specs += [pl.BlockSpec((tm, LANE), lambda b, i: (i, 0))] * len(tables)
    ins += consts
    in_specs += [full(a) for a in consts]
    widths = [MLA_HEADS * MLA_QK_PAD, MLA_HEADS * MLA_QK_PAD, MLA_WIDTH, SWA_WIDTH,
              SWA_KV_HEADS * SWA_HEAD_DIM, DIFF_WIDTH, DIFF_WIDTH]
    return pl.pallas_call(
        functools.partial(_prep_kernel, rope=rope),
        grid=(bx, tx // tm),
        in_specs=in_specs,
        out_specs=[pl.BlockSpec((1, tm, w), lambda b, i: (b, i, 0)) for w in widths],
        out_shape=[jax.ShapeDtypeStruct((bx, tx, w), BF16) for w in widths],
        compiler_params=_cparams(("parallel", "parallel")),
        name="head_prep_rope" if rope else "head_prep",
    )(*ins)


def _fold_lanes(t, op):
    parts = [t[:, c * LANE:(c + 1) * LANE] for c in range(t.shape[1] // LANE)]
    return functools.reduce(op, parts)


def _key_chunks(srcs, tk):
    out = []
    for k_ref, v_ref in srcs:
        n = k_ref.shape[0]
        for c0 in range(0, n, tk):
            out.append((k_ref, v_ref, c0, min(tk, n - c0)))
    return out


def _dense_attention(q, srcs, s_scr, tk, sink=None):
    chunks = _key_chunks(srcs, tk)
    mpart = None
    off = 0
    for k_ref, _, c0, n in chunks:
        s = lax.dot_general(q, k_ref[c0:c0 + n, :], (((1,), (1,)), ((), ())), preferred_element_type=F32)
        s_scr[:, off:off + n] = s
        f = _fold_lanes(s, jnp.maximum)
        mpart = f if mpart is None else jnp.maximum(mpart, f)
        off += n
    m = jnp.max(mpart, axis=-1, keepdims=True)
    if sink is not None:
        m = jnp.maximum(m, sink)
    lpart = None
    acc = None
    off = 0
    for _, v_ref, c0, n in chunks:
        p = jnp.exp2(s_scr[:, off:off + n] - m)
        f = _fold_lanes(p, jnp.add)
        lpart = f if lpart is None else lpart + f
        pv = jnp.dot(p.astype(BF16), v_ref[c0:c0 + n, :], preferred_element_type=F32)
        acc = pv if acc is None else acc + pv
        off += n
    l = jnp.sum(lpart, axis=-1, keepdims=True)
    if sink is not None:
        l = l + jnp.exp2(sink - m)
    return acc * (1.0 / l)


class _View:
    def __init__(self, ref, lead):
        self.ref, self.lead = ref, lead
        self.shape = ref.shape[len(lead):]

    def __getitem__(self, idx):
        return self.ref[self.lead + idx]


def _score_pass(q, k_views, s_ref, m_ref, tk):
    mpart = None
    off = 0
    for k_ref in k_views:
        n_keys = k_ref.shape[0]
        for c0 in range(0, n_keys, tk):
            n = min(tk, n_keys - c0)
            s = lax.dot_general(q, k_ref[c0:c0 + n, :], (((1,), (1,)), ((), ())), preferred_element_type=F32)
            s_ref[:, off:off + n] = s
            f = _fold_lanes(s, jnp.maximum)
            mpart = f if mpart is None else jnp.maximum(mpart, f)
            off += n
            yield
    m_ref[...] = jnp.broadcast_to(jnp.max(mpart, axis=-1, keepdims=True), m_ref.shape)
    yield


def _prob_pass(s_ref, m_ref, v_views, tk, result):
    mb = m_ref[...]
    lpart = None
    acc = None
    off = 0
    for v_ref in v_views:
        n_keys = v_ref.shape[0]
        for c0 in range(0, n_keys, tk):
            n = min(tk, n_keys - c0)
            ps = []
            for c in range(n // LANE):
                pc = jnp.exp2(s_ref[:, off + c * LANE:off + (c + 1) * LANE] - mb)
                lpart = pc if lpart is None else lpart + pc
                ps.append(pc.astype(BF16))
            pv = jnp.dot(jnp.concatenate(ps, axis=1), v_ref[c0:c0 + n, :], preferred_element_type=F32)
            acc = pv if acc is None else acc + pv
            off += n
            yield
    result.append(acc * (1.0 / jnp.sum(lpart, axis=-1, keepdims=True)))
    yield


def _interleave(*gens):
    live = list(gens)
    while live:
        for g in list(live):
            try:
                next(g)
            except StopIteration:
                live.remove(g)


def _pipelined_attention_kernel(*refs, n_src, n_extra, tk, n_units, load_q, finish):
    q_ref = refs[0]
    k_views = [_View(r, (0,)) for r in refs[1:1 + n_src]]
    v_views = [_View(r, (0,)) for r in refs[1 + n_src:1 + 2 * n_src]]
    extra = refs[1 + 2 * n_src:1 + 2 * n_src + n_extra]
    o_ref, s_a, s_b, m_a, m_b = refs[1 + 2 * n_src + n_extra:]
    t = pl.program_id(0)
    bufs = ((s_a, m_a), (s_b, m_b))

    def score(slot):
        return _score_pass(load_q(q_ref), k_views, bufs[slot][0], bufs[slot][1], tk)

    def prob(slot):
        result = []
        yield from _prob_pass(bufs[slot][0], bufs[slot][1], v_views, tk, result)
        finish(result[0], o_ref, extra)

    @pl.when(t == 0)
    def _():
        _interleave(score(0))

    for parity in range(2):
        @pl.when((t > 0) & (t < n_units) & (t % 2 == parity))
        def _():
            _interleave(score(parity), prob(1 - parity))

    @pl.when(t == n_units)
    def _():
        _interleave(prob((n_units - 1) % 2))


def _pipelined_attention(q, q_width, k_list, v_list, extra, heads, out_width, tq, tk, rows_per_q, load_q, finish, name):
    b, t_q, _ = q.shape
    nq = t_q // tq
    n_units = b * heads * nq

    def unit(t):
        u = jnp.minimum(t, n_units - 1)
        return u // (heads * nq), (u // nq) % heads, u % nq

    def q_map(t):
        bi, h, i = unit(t)
        return bi, i, h

    def o_map(t):
        bi, h, i = unit(jnp.maximum(t - 1, 0))
        return bi, i, h

    def kv_map(col0, lag):
        def f(t):
            bi, h, _ = unit(jnp.maximum(t - lag, 0))
            return bi, 0, col0 + h
        return f

    ins = [q]
    in_specs = [pl.BlockSpec((1, tq, q_width), q_map)]
    total = 0
    for arr, width, col0 in k_list:
        total += arr.shape[1]
        ins.append(arr)
        in_specs.append(pl.BlockSpec((1, arr.shape[1], width), kv_map(col0, 0)))
    for arr, width, col0 in v_list:
        ins.append(arr)
        in_specs.append(pl.BlockSpec((1, arr.shape[1], width), kv_map(col0, 1)))
    ins += list(extra)
    in_specs += [pl.BlockSpec(a.shape, lambda t: (0, 0)) for a in extra]
    m_rows = rows_per_q * tq
    return pl.pallas_call(
        functools.partial(_pipelined_attention_kernel, n_src=len(k_list), n_extra=len(extra), tk=tk,
                          n_units=n_units, load_q=load_q, finish=finish),
        grid=(n_units + 1,),
        in_specs=in_specs,
        out_specs=pl.BlockSpec((1, tq, out_width), o_map),
        out_shape=jax.ShapeDtypeStruct((b, t_q, heads * out_width), BF16),
        scratch_shapes=[pltpu.VMEM((m_rows, total), F32), pltpu.VMEM((m_rows, total), F32),
                        pltpu.VMEM((m_rows, LANE), F32), pltpu.VMEM((m_rows, LANE), F32)],
        compiler_params=_cparams(("arbitrary",)),
        name=name,
    )(*ins)


def _mla_load_q(q_ref):
    return q_ref[0]


def _mla_finish(o, o_ref, extra):
    o_ref[0] = o.astype(BF16)


def _mla_attention(q, kv_list, tq, tk):
    return _pipelined_attention(
        q, MLA_QK_PAD, [(k, MLA_QK_PAD, 0) for k, _ in kv_list], [(v, MLA_V, 0) for _, v in kv_list], (),
        MLA_HEADS, MLA_V, tq, tk, 1, _mla_load_q, _mla_finish, "mla_attention")


def _diff_load_q(q_ref):
    q = q_ref[0]
    first = lax.broadcasted_iota(jnp.int32, q.shape, 1) < DIFF_QK_DIM
    zero = jnp.zeros_like(q)
    return jnp.concatenate([jnp.where(first, q, zero), jnp.where(first, zero, q)], axis=0)


def _diff_finish(o, o_ref, extra, lam_init):
    lq1_ref, lk1_ref, lq2_ref, lk2_ref, og_ref = extra
    tq = o.shape[0] // 2
    lam = (jnp.exp(jnp.sum(lq1_ref[...] * lk1_ref[...], axis=-1, keepdims=True))
           - jnp.exp(jnp.sum(lq2_ref[...] * lk2_ref[...], axis=-1, keepdims=True)) + lam_init)
    d = o[:tq] - lam * o[tq:]
    r = lax.rsqrt(jnp.mean(d * d, axis=-1, keepdims=True) + EPS)
    o_ref[0] = (((d * r) * og_ref[...]) * (1.0 - lam_init)).astype(BF16)


def _diff_attention(q, kv_list, v_col0, lams, out_gain, lam_init, tq, tk):
    extra = [a.reshape(1, -1) for a in lams] + [out_gain.reshape(1, -1)]
    return _pipelined_attention(
        q, LANE, [(k, LANE, 0) for k, _ in kv_list], [(pv, LANE, v_col0) for _, pv in kv_list], extra,
        DIFF_HEADS, DIFF_V_DIM, tq, tk, 2, _diff_load_q, functools.partial(_diff_finish, lam_init=lam_init),
        "diff_attention")


def _stack_heads(q_ref):
    return jnp.concatenate([q_ref[0, :, g * LANE:(g + 1) * LANE] for g in range(SWA_GROUP)], axis=0)


def _sink_rows(sink_ref, h, tq):
    return jnp.concatenate(
        [jnp.full((tq, 1), sink_ref[h * SWA_GROUP + g] * LOG2E, F32) for g in range(SWA_GROUP)], axis=0)


def _swa_ctx_kernel(sink_ref, q_ref, k_ref, v_ref, o_ref, s_scr):
    tq = q_ref.shape[1]
    o = _dense_attention(_stack_heads(q_ref), [(_View(k_ref, (0,)), _View(v_ref, (0,)))], s_scr, k_ref.shape[1],
                         sink=_sink_rows(sink_ref, pl.program_id(1), tq))
    for g in range(SWA_GROUP):
        o_ref[0, :, g * LANE:(g + 1) * LANE] = o[g * tq:(g + 1) * tq].astype(BF16)


def _swa_window_kernel(sink_ref, q_ref, k_ref, v_ref, kc_ref, vc_ref, o_ref):
    h, i = pl.program_id(1), pl.program_id(2)
    tq = q_ref.shape[1]
    s_len = k_ref.shape[1]
    win = tq + 2 * SWA_WINDOW
    start = pl.multiple_of(jnp.clip(i * tq - SWA_WINDOW, 0, s_len - win), LANE)
    qq = _stack_heads(q_ref)
    kw = k_ref[0, pl.ds(start, win), :]
    vw = v_ref[0, pl.ds(start, win), :]
    dn = (((1,), (1,)), ((), ()))
    s_w = lax.dot_general(qq, kw, dn, preferred_element_type=F32)
    s_c = lax.dot_general(qq, kc_ref[0], dn, preferred_element_type=F32)
    rel = (lax.broadcasted_iota(jnp.int32, (tq, win), 1) - lax.broadcasted_iota(jnp.int32, (tq, win), 0)
           + (start - i * tq))
    bias = jnp.where(jnp.abs(rel) <= SWA_WINDOW, 0.0, NEG_BIG).astype(F32)
    s_w = s_w + jnp.concatenate([bias] * SWA_GROUP, axis=0)
    sink = _sink_rows(sink_ref, h, tq)
    m = jnp.maximum(jnp.maximum(jnp.max(s_w, axis=-1, keepdims=True), jnp.max(s_c, axis=-1, keepdims=True)), sink)
    p_w = jnp.exp2(s_w - m)
    p_c = jnp.exp2(s_c - m)
    l = jnp.sum(p_w, axis=-1, keepdims=True) + jnp.sum(p_c, axis=-1, keepdims=True) + jnp.exp2(sink - m)
    o = (jnp.dot(p_w.astype(BF16), vw, preferred_element_type=F32)
         + jnp.dot(p_c.astype(BF16), vc_ref[0], preferred_element_type=F32)) * (1.0 / l)
    for g in range(SWA_GROUP):
        o_ref[0, :, g * LANE:(g + 1) * LANE] = o[g * tq:(g + 1) * tq].astype(BF16)


def _swa_attention(q, k, pv, kc, pvc, sink, tq):
    b, s_len, _ = q.shape
    lc = kc.shape[1]
    vblk = C_SWA_V // LANE
    qspec = pl.BlockSpec((1, tq, SWA_GROUP * LANE), lambda bi, h, i: (bi, i, h))
    return pl.pallas_call(
        _swa_window_kernel,
        grid=(b, SWA_KV_HEADS, s_len // tq),
        in_specs=[
            pl.BlockSpec(memory_space=pltpu.SMEM),
            qspec,
            pl.BlockSpec((1, s_len, LANE), lambda bi, h, i: (bi, 0, h)),
            pl.BlockSpec((1, s_len, LANE), lambda bi, h, i: (bi, 0, vblk + h)),
            pl.BlockSpec((1, lc, LANE), lambda bi, h, i: (bi, 0, h)),
            pl.BlockSpec((1, lc, LANE), lambda bi, h, i: (bi, 0, vblk + h)),
        ],
        out_specs=qspec,
        out_shape=jax.ShapeDtypeStruct((b, s_len, SWA_WIDTH), BF16),
        compiler_params=_cparams(("parallel", "parallel", "arbitrary")),
        name="swa_window_attention",
    )(sink, q, k, pv, kc, pvc)


def _swa_ctx_attention(q, k, pv, sink):
    b, lc, _ = q.shape
    vblk = C_SWA_V // LANE
    qspec = pl.BlockSpec((1, lc, SWA_GROUP * LANE), lambda bi, h: (bi, 0, h))
    return pl.pallas_call(
        _swa_ctx_kernel,
        grid=(b, SWA_KV_HEADS),
        in_specs=[
            pl.BlockSpec(memory_space=pltpu.SMEM),
            qspec,
            pl.BlockSpec((1, lc, LANE), lambda bi, h: (bi, 0, h)),
            pl.BlockSpec((1, lc, LANE), lambda bi, h: (bi, 0, vblk + h)),
        ],
        out_specs=qspec,
        out_shape=jax.ShapeDtypeStruct((b, lc, SWA_WIDTH), BF16),
        scratch_shapes=[pltpu.VMEM((SWA_GROUP * lc, lc), F32)],
        compiler_params=_cparams(("parallel", "parallel")),
        name="swa_ctx_attention",
    )(sink, q, k, pv)


def _outproj_kernel(ya_ref, yb_ref, yc_ref, gate_ref, h_ref, g_ref, w_ref, o_ref):
    y = jnp.concatenate([ya_ref[0], yb_ref[0], yc_ref[0]], axis=-1).astype(F32)
    gt = gate_ref[0].astype(F32)
    a = (y * (gt * (1.0 / (1.0 + jnp.exp(-gt))))).astype(BF16)
    o_ref[0] = h_ref[0] + g_ref[0] * jnp.dot(a, w_ref[...], preferred_element_type=F32)


def _outproj(ya, yb, yc, p, h, g, w, tm):
    bx, tx, d = h.shape
    tok = lambda wd: pl.BlockSpec((1, tm, wd), lambda b, i: (b, i, 0))
    return pl.pallas_call(
        _outproj_kernel,
        grid=(bx, tx // tm),
        in_specs=[
            tok(MLA_WIDTH), tok(SWA_WIDTH), tok(DIFF_WIDTH),
            pl.BlockSpec((1, tm, MIX_WIDTH), lambda b, i: (b, i, C_GATE // MIX_WIDTH)),
            tok(d),
            pl.BlockSpec((1, 1, d), lambda b, i: (b, 0, 0)),
            pl.BlockSpec(w.shape, lambda b, i: (0, 0)),
        ],
        out_specs=tok(d),
        out_shape=jax.ShapeDtypeStruct((bx, tx, d), F32),
        compiler_params=_cparams(("parallel", "parallel")),
        name="gated_outproj_residual",
    )(ya, yb, yc, p, h, g, w)


def _layer_weights(l, w_in, mla_w_uq, mla_w_ukv, mla_q_gain, mla_k_gain, w_out):
    d = w_in.shape[1]
    wi = w_in[l]
    o = np.cumsum([0, MLA_Q_RANK, MLA_KV_RANK, MLA_ROPE, MLA_WIDTH, SWA_WIDTH, SWA_KV_HEADS * SWA_HEAD_DIM,
                   SWA_KV_HEADS * SWA_HEAD_DIM, SWA_WIDTH, DIFF_WIDTH, DIFF_WIDTH, DIFF_WIDTH, DIFF_WIDTH]).tolist()
    seg = lambda j: wi[:, o[j]:o[j + 1]]
    z = lambda n: jnp.zeros((d, n), wi.dtype)
    w_p = jnp.concatenate([seg(0), seg(1), seg(2), z(C_SWA_Q - C_KR - MLA_ROPE),
                           seg(4), seg(5), seg(6), seg(8), seg(9), seg(10), z(C_QKV_END - C_DIF_V - DIFF_WIDTH),
                           seg(3), seg(7), seg(11)], axis=1).astype(BF16)
    pad = MLA_QK_PAD - MLA_QK
    w_uq = jnp.pad(mla_w_uq[l].reshape(MLA_Q_RANK, MLA_HEADS, MLA_QK), ((0, 0), (0, 0), (0, pad)))
    w_ukv = mla_w_ukv[l].reshape(MLA_KV_RANK, MLA_HEADS, MLA_NOPE + MLA_V)
    return {
        "w_p": w_p,
        "w_uq": w_uq.reshape(MLA_Q_RANK, MLA_HEADS * MLA_QK_PAD).astype(BF16),
        "w_kn": w_ukv[:, :, :MLA_NOPE].reshape(MLA_KV_RANK, MLA_HEADS * MLA_NOPE).astype(BF16),
        "w_v": w_ukv[:, :, MLA_NOPE:].reshape(MLA_KV_RANK, MLA_WIDTH).astype(BF16),
        "q_gain": jnp.pad(mla_q_gain[l], (0, pad)),
        "k_gain_nope": mla_k_gain[l][:MLA_NOPE],
        "k_gain_rope": jnp.pad(mla_k_gain[l][MLA_NOPE:], (0, LANE - MLA_ROPE)),
        "w_out": w_out[l].astype(BF16),
    }


def _rope_tables(n_tokens):
    t = np.arange(n_tokens)
    row, col = (t // GRID_W).astype(np.float64), (t % GRID_W).astype(np.float64)

    def cos_sin(rot_dim):
        n_freq = rot_dim // 4
        inv = np.power(ROPE_BASE, -np.arange(n_freq, dtype=np.float32) / n_freq).astype(np.float32)
        ang = np.concatenate([row[:, None].astype(np.float32) * inv, col[:, None].astype(np.float32) * inv], axis=-1)
        return np.cos(ang).astype(np.float32), np.sin(ang).astype(np.float32)

    c64, s64 = cos_sin(MLA_ROPE)
    c128, s128 = cos_sin(SWA_HEAD_DIM)
    zeros = np.zeros_like(s64)
    cd = np.concatenate([c64] * 4, axis=-1)
    sneg = np.concatenate([-s64, zeros, -s64, zeros], axis=-1)
    spos = np.concatenate([zeros, s64, zeros, s64], axis=-1)
    cs = np.concatenate([c128, c128], axis=-1)
    ss = np.concatenate([-s128, s128], axis=-1)
    return tuple(jnp.asarray(a, F32) for a in (cd, sneg, spos, cs, ss))


def _block_diag_ones(n, seg):
    idx = np.arange(n) // seg
    return jnp.asarray(idx[:, None] == idx[None, :], BF16)


def _pick(n, prefs):
    for t in prefs:
        if n % t == 0:
            return t
    return n


def kernel(x, c, ctx, c_ctx, norm_g, w_ada, b_ada, w_in, mla_q_norm, mla_w_uq, mla_kv_norm, mla_w_ukv,
           mla_q_gain, mla_k_gain, swa_q_gain, swa_k_gain, swa_sink, dif_q_gain, dif_k_gain,
           dif_lq1, dif_lk1, dif_lq2, dif_lk2, dif_out_gain, w_out):
    b, s, d = x.shape
    lc = ctx.shape[1]
    depth = w_in.shape[0]
    assert s % (2 * LANE) == 0 and s >= 4 * LANE and lc % LANE == 0 and d == MIX_WIDTH

    tables = _rope_tables(s)
    bd = _block_diag_ones(DIFF_WIDTH, DIFF_QK_DIM)

    n_rows = -(-(b + 1) // 8) * 8
    cvec = jnp.concatenate([c, c_ctx[None, :], jnp.zeros((n_rows - b - 1, d), F32)], axis=0)
    mod = _modulation(cvec, w_ada, b_ada)

    hx = x
    hc = ctx.reshape(1, b * lc, d)
    tm_x = _pick(s, (1024, 512, 256))
    tm_c = _pick(b * lc, (1024, 512, 256))
    tq_a = _pick(s, (512, 256))
    tq_b = _pick(s, (256,))
    tq_c = _pick(s, (256,))
    tk = 512

    for l in range(depth):
        need_ctx_out = l < depth - 1
        wl = _layer_weights(l, w_in, mla_w_uq, mla_w_ukv, mla_q_gain, mla_k_gain, w_out)
        wl.update(mla_q_norm=mla_q_norm[l], mla_kv_norm=mla_kv_norm[l], swa_q_gain=swa_q_gain[l],
                  swa_k_gain=swa_k_gain[l], dif_q_gain=jnp.tile(dif_q_gain[l], DIFF_WIDTH // DIFF_QK_DIM),
                  dif_k_gain=jnp.tile(dif_k_gain[l], DIFF_WIDTH // DIFF_QK_DIM), bd=bd)
        m = mod[l]
        sh_x, sc_x, g_x = (m[:b, j * d:(j + 1) * d].reshape(b, 1, d) for j in range(3))
        sh_c, sc_c, g_c = (m[b:b + 1, j * d:(j + 1) * d].reshape(1, 1, d) for j in range(3))
        ng = norm_g[l].reshape(1, d)
        lam_init = 0.8 - 0.6 * math.exp(-0.3 * l)
        lams = (dif_lq1[l], dif_lk1[l], dif_lq2[l], dif_lk2[l])
        sink = swa_sink[l]

        px = _inproj(hx, sc_x, sh_x, ng, wl["w_p"], tm_x, 1024)
        pc = _inproj(hc, sc_c, sh_c, ng, wl["w_p"], tm_c, 1024)
        qa_x, ka_x, va_x, qb_x, kb_x, qc_x, kc_x = _prep(px, tables, wl, _pick(s, (512, 256)))
        qa_c, ka_c, va_c, qb_c, kb_c, qc_c, kc_c = (
            t.reshape(b, lc, -1) for t in _prep(pc, None, wl, _pick(b * lc, (512, 256))))
        pcb = pc.reshape(b, lc, P_COLS)
        vblk_c = C_DIF_V // LANE

        ya_x = _mla_attention(qa_x, [(ka_x, va_x), (ka_c, va_c)], tq_a, tk)
        yb_x = _swa_attention(qb_x, kb_x, px, kb_c, pcb, sink, tq_b)
        yc_x = _diff_attention(qc_x, [(kc_x, px), (kc_c, pcb)], vblk_c, lams, dif_out_gain[l], lam_init, tq_c, tk)

        if need_ctx_out:
            ya_c = _mla_attention(qa_c, [(ka_c, va_c)], lc, tk)
            yb_c = _swa_ctx_attention(qb_c, kb_c, pcb, sink)
            yc_c = _diff_attention(qc_c, [(kc_c, pcb)], vblk_c, lams, dif_out_gain[l], lam_init, lc, tk)
            flat = lambda t: t.reshape(1, b * lc, -1)
            hc = _outproj(flat(ya_c), flat(yb_c), flat(yc_c), pc, hc, g_c, wl["w_out"], _pick(b * lc, (512, 256)))

        hx = _outproj(ya_x, yb_x, yc_x, px, hx, g_x, wl["w_out"], _pick(s, (512, 256)))

    return hx
```

```python
import functools
import math

import numpy as np
import jax
import jax.numpy as jnp
from jax import lax
from jax.experimental import pallas as pl
from jax.experimental.pallas import tpu as pltpu

F32 = jnp.float32
BF16 = jnp.bfloat16

GRID_W = 64
ROPE_BASE = 10000.0
EPS = 1e-6
LANE = 128

MLA_HEADS = 6
MLA_Q_RANK = 512
MLA_KV_RANK = 256
MLA_NOPE = 128
MLA_ROPE = 64
MLA_V = 128
MLA_QK = MLA_NOPE + MLA_ROPE
MLA_QK_PAD = 2 * LANE
MLA_WIDTH = MLA_HEADS * MLA_V

SWA_Q_HEADS = 6
SWA_KV_HEADS = 2
SWA_GROUP = SWA_Q_HEADS // SWA_KV_HEADS
SWA_HEAD_DIM = 128
SWA_WINDOW = 128
SWA_WIDTH = SWA_Q_HEADS * SWA_HEAD_DIM

DIFF_HEADS = 4
DIFF_QK_DIM = 64
DIFF_V_DIM = 2 * DIFF_QK_DIM
DIFF_WIDTH = DIFF_HEADS * DIFF_V_DIM

MIX_WIDTH = MLA_WIDTH + SWA_WIDTH + DIFF_WIDTH

LOG2E = math.log2(math.e)
NEG_BIG = -1e30

C_CQ = 0
C_CKV = C_CQ + MLA_Q_RANK
C_KR = C_CKV + MLA_KV_RANK
C_SWA_Q = 1024
C_SWA_K = C_SWA_Q + SWA_WIDTH
C_SWA_V = C_SWA_K + SWA_KV_HEADS * SWA_HEAD_DIM
C_DIF_Q = C_SWA_V + SWA_KV_HEADS * SWA_HEAD_DIM
C_DIF_K = C_DIF_Q + DIFF_WIDTH
C_DIF_V = C_DIF_K + DIFF_WIDTH
C_QKV_END = 4096
C_GATE = C_QKV_END
P_COLS = C_GATE + MIX_WIDTH

VMEM_LIMIT = 56 * 1024 * 1024


def _cparams(sem):
    return pltpu.CompilerParams(dimension_semantics=sem, vmem_limit_bytes=VMEM_LIMIT)


def _mod_kernel(c_ref, w_ref, b_ref, o_ref):
    c = c_ref[...]
    a = c * (1.0 / (1.0 + jnp.exp(-c)))
    o_ref[0] = jnp.dot(a, w_ref[0], preferred_element_type=F32, precision=lax.Precision.HIGHEST) + b_ref[0]


def _modulation(cvec, w_ada, b_ada):
    depth, d, n3 = w_ada.shape
    rows = cvec.shape[0]
    tn = 1536
    return pl.pallas_call(
        _mod_kernel,
        grid=(depth, n3 // tn),
        in_specs=[
            pl.BlockSpec((rows, d), lambda l, n: (0, 0)),
            pl.BlockSpec((1, d, tn), lambda l, n: (l, 0, n)),
            pl.BlockSpec((1, 1, tn), lambda l, n: (l, 0, n)),
        ],
        out_specs=pl.BlockSpec((1, rows, tn), lambda l, n: (l, 0, n)),
        out_shape=jax.ShapeDtypeStruct((depth, rows, n3), F32),
        compiler_params=_cparams(("parallel", "parallel")),
        name="adaln_modulation",
    )(cvec, w_ada, b_ada.reshape(depth, 1, n3))


def _inproj_kernel(x_ref, sc_ref, sh_ref, g_ref, w_ref, o_ref, xn_ref):
    @pl.when(pl.program_id(2) == 0)
    def _():
        x = x_ref[0]
        r = lax.rsqrt(jnp.mean(x * x, axis=-1, keepdims=True) + EPS)
        y = (x * r) * g_ref[...]
        xn_ref[...] = (y * (1.0 + sc_ref[0]) + sh_ref[0]).astype(BF16)

    o_ref[0] = jnp.dot(xn_ref[...], w_ref[...], preferred_element_type=F32).astype(BF16)


def _inproj(h, sc, sh, norm_g, w, tm, tn):
    bx, tx, d = h.shape
    n = w.shape[1]
    return pl.pallas_call(
        _inproj_kernel,
        grid=(bx, tx // tm, n // tn),
        in_specs=[
            pl.BlockSpec((1, tm, d), lambda b, i, j: (b, i, 0)),
            pl.BlockSpec((1, 1, d), lambda b, i, j: (b, 0, 0)),
            pl.BlockSpec((1, 1, d), lambda b, i, j: (b, 0, 0)),
            pl.BlockSpec((1, d), lambda b, i, j: (0, 0)),
            pl.BlockSpec((d, tn), lambda b, i, j: (0, j)),
        ],
        out_specs=pl.BlockSpec((1, tm, tn), lambda b, i, j: (b, i, j)),
        out_shape=jax.ShapeDtypeStruct((bx, tx, n), BF16),
        scratch_shapes=[pltpu.VMEM((tm, d), BF16)],
        compiler_params=_cparams(("parallel", "parallel", "arbitrary")),
        name="norm_mod_inproj",
    )(h, sc, sh, norm_g, w)


def _rope64(x, cd, sneg, spos):
    return x * cd + pltpu.roll(x, 96, 1) * sneg + pltpu.roll(x, 32, 1) * spos


def _rope128(x, cs, ss):
    return x * cs + pltpu.roll(x, 64, 1) * ss


def _prep_kernel(*refs, rope):
    if rope:
        (p_ref, cd_ref, sneg_ref, spos_ref, cs_ref, ss_ref,
         qn_g_ref, wuq_ref, qgain_ref, kvn_g_ref, wkn_ref, wv_ref, kgn_ref, kgr_ref,
         sqg_ref, skg_ref, dqg_ref, dkg_ref, bd_ref,
         qa_ref, ka_ref, va_ref, qb_ref, kb_ref, qc_ref, kc_ref) = refs
        cd, sneg, spos = cd_ref[...], sneg_ref[...], spos_ref[...]
        cs, ss = cs_ref[...], ss_ref[...]
        rope64 = lambda t: _rope64(t, cd, sneg, spos)
        rope128 = lambda t: _rope128(t, cs, ss)
    else:
        (p_ref, qn_g_ref, wuq_ref, qgain_ref, kvn_g_ref, wkn_ref, wv_ref, kgn_ref, kgr_ref,
         sqg_ref, skg_ref, dqg_ref, dkg_ref, bd_ref,
         qa_ref, ka_ref, va_ref, qb_ref, kb_ref, qc_ref, kc_ref) = refs
        rope64 = rope128 = lambda t: t

    def cols(lo, n):
        return p_ref[0, :, lo:lo + n].astype(F32)

    def rms(t, width):
        return lax.rsqrt(jnp.sum(t * t, axis=-1, keepdims=True) * (1.0 / width) + EPS)

    cq = cols(C_CQ, MLA_Q_RANK)
    cqn = ((cq * rms(cq, MLA_Q_RANK)) * qn_g_ref[...]).astype(BF16)
    q = jnp.dot(cqn, wuq_ref[...], preferred_element_type=F32)
    qscale = MLA_QK ** -0.5 * LOG2E
    for h in range(MLA_HEADS):
        qh = q[:, h * MLA_QK_PAD:(h + 1) * MLA_QK_PAD]
        qh = (qh * rms(qh, MLA_QK)) * qgain_ref[...]
        qa_ref[0, :, h * MLA_QK_PAD:h * MLA_QK_PAD + LANE] = (qh[:, :LANE] * qscale).astype(BF16)
        qa_ref[0, :, h * MLA_QK_PAD + LANE:(h + 1) * MLA_QK_PAD] = (rope64(qh[:, LANE:]) * qscale).astype(BF16)

    ckv = cols(C_CKV, MLA_KV_RANK)
    ckvn = ((ckv * rms(ckv, MLA_KV_RANK)) * kvn_g_ref[...]).astype(BF16)
    kn = jnp.dot(ckvn, wkn_ref[...], preferred_element_type=F32)
    va_ref[0] = jnp.dot(ckvn, wv_ref[...], preferred_element_type=F32).astype(BF16)
    krp = cols(C_KR, LANE)
    kr_ssq = jnp.sum(krp * krp, axis=-1, keepdims=True)
    kr_rot = rope64(krp * kgr_ref[...])
    for h in range(MLA_HEADS):
        knh = kn[:, h * LANE:(h + 1) * LANE]
        r = lax.rsqrt((jnp.sum(knh * knh, axis=-1, keepdims=True) + kr_ssq) * (1.0 / MLA_QK) + EPS)
        ka_ref[0, :, h * MLA_QK_PAD:h * MLA_QK_PAD + LANE] = ((knh * r) * kgn_ref[...]).astype(BF16)
        ka_ref[0, :, h * MLA_QK_PAD + LANE:(h + 1) * MLA_QK_PAD] = (kr_rot * r).astype(BF16)

    bscale = SWA_HEAD_DIM ** -0.5 * LOG2E
    for h in range(SWA_Q_HEADS):
        t = cols(C_SWA_Q + h * LANE, LANE)
        t = rope128((t * rms(t, SWA_HEAD_DIM)) * sqg_ref[...])
        qb_ref[0, :, h * LANE:(h + 1) * LANE] = (t * bscale).astype(BF16)
    for h in range(SWA_KV_HEADS):
        t = cols(C_SWA_K + h * LANE, LANE)
        kb_ref[0, :, h * LANE:(h + 1) * LANE] = rope128((t * rms(t, SWA_HEAD_DIM)) * skg_ref[...]).astype(BF16)

    def seg_norm(t, gain):
        t2 = t * t
        hi = t2.astype(BF16)
        lo = (t2 - hi.astype(F32)).astype(BF16)
        ssq = (jnp.dot(hi, bd_ref[...], preferred_element_type=F32)
               + jnp.dot(lo, bd_ref[...], preferred_element_type=F32))
        return (t * lax.rsqrt(ssq * (1.0 / DIFF_QK_DIM) + EPS)) * gain

    cscale = DIFF_QK_DIM ** -0.5 * LOG2E
    tq = seg_norm(cols(C_DIF_Q, DIFF_WIDTH), dqg_ref[...])
    tk = seg_norm(cols(C_DIF_K, DIFF_WIDTH), dkg_ref[...])
    for j in range(DIFF_WIDTH // LANE):
        sl = slice(j * LANE, (j + 1) * LANE)
        qc_ref[0, :, sl] = (rope64(tq[:, sl]) * cscale).astype(BF16)
        kc_ref[0, :, sl] = rope64(tk[:, sl]).astype(BF16)


def _prep(p, tables, wl, tm):
    bx, tx, _ = p.shape
    rope = tables is not None
    row = lambda a: a.reshape(1, -1)
    full = lambda a: pl.BlockSpec(a.shape, lambda b, i: (0,) * a.ndim)
    consts = [row(wl["mla_q_norm"]), wl["w_uq"], row(wl["q_gain"]), row(wl["mla_kv_norm"]), wl["w_kn"], wl["w_v"],
              row(wl["k_gain_nope"]), row(wl["k_gain_rope"]), row(wl["swa_q_gain"]), row(wl["swa_k_gain"]),
              row(wl["dif_q_gain"]), row(wl["dif_k_gain"]), wl["bd"]]
    ins = [p]
    in_specs = [pl.BlockSpec((1, tm, C_QKV_END), lambda b, i: (b, i, 0))]
    if rope:
        ins += list(tables)
        in_specs += [pl.BlockSpec((tm, LANE), lambda b, i: (i, 0))] * len(tables)
    ins += consts
    in_specs += [full(a) for a in consts]
    widths = [MLA_HEADS * MLA_QK_PAD, MLA_HEADS * MLA_QK_PAD, MLA_WIDTH, SWA_WIDTH,
              SWA_KV_HEADS * SWA_HEAD_DIM, DIFF_WIDTH, DIFF_WIDTH]
    return pl.pallas_call(
        functools.partial(_prep_kernel, rope=rope),
        grid=(bx, tx // tm),
        in_specs=in_specs,
        out_specs=[pl.BlockSpec((1, tm, w), lambda b, i: (b, i, 0)) for w in widths],
        out_shape=[jax.ShapeDtypeStruct((bx, tx, w), BF16) for w in widths],
        compiler_params=_cparams(("parallel", "parallel")),
        name="head_prep_rope" if rope else "head_prep",
    )(*ins)


def _fold_lanes(t, op):
    parts = [t[:, c * LANE:(c + 1) * LANE] for c in range(t.shape[1] // LANE)]
    return functools.reduce(op, parts)


def _key_chunks(srcs, tk):
    out = []
    for k_ref, v_ref in srcs:
        n = k_ref.shape[0]
        for c0 in range(0, n, tk):
            out.append((k_ref, v_ref, c0, min(tk, n - c0)))
    return out


def _dense_attention(q, srcs, s_scr, tk, sink=None):
    chunks = _key_chunks(srcs, tk)
    mpart = None
    off = 0
    for k_ref, _, c0, n in chunks:
        s = lax.dot_general(q, k_ref[c0:c0 + n, :], (((1,), (1,)), ((), ())), preferred_element_type=F32)
        s_scr[:, off:off + n] = s
        f = _fold_lanes(s, jnp.maximum)
        mpart = f if mpart is None else jnp.maximum(mpart, f)
        off += n
    m = jnp.max(mpart, axis=-1, keepdims=True)
    if sink is not None:
        m = jnp.maximum(m, sink)
    lpart = None
    acc = None
    off = 0
    for _, v_ref, c0, n in chunks:
        p = jnp.exp2(s_scr[:, off:off + n] - m)
        f = _fold_lanes(p, jnp.add)
        lpart = f if lpart is None else lpart + f
        pv = jnp.dot(p.astype(BF16), v_ref[c0:c0 + n, :], preferred_element_type=F32)
        acc = pv if acc is None else acc + pv
        off += n
    l = jnp.sum(lpart, axis=-1, keepdims=True)
    if sink is not None:
        l = l + jnp.exp2(sink - m)
    return acc * (1.0 / l)


class _View:
    def __init__(self, ref, lead):
        self.ref, self.lead = ref, lead
        self.shape = ref.shape[len(lead):]

    def __getitem__(self, idx):
        return self.ref[self.lead + idx]


def _score_pass(q, k_views, s_ref, m_ref, tk):
    mpart = None
    off = 0
    for k_ref in k_views:
        n_keys = k_ref.shape[0]
        for c0 in range(0, n_keys, tk):
            n = min(tk, n_keys - c0)
            s = lax.dot_general(q, k_ref[c0:c0 + n, :], (((1,), (1,)), ((), ())), preferred_element_type=F32)
            s_ref[:, off:off + n] = s
            f = _fold_lanes(s, jnp.maximum)
            mpart = f if mpart is None else jnp.maximum(mpart, f)
            off += n
            yield
    m_ref[...] = jnp.broadcast_to(jnp.max(mpart, axis=-1, keepdims=True), m_ref.shape)
    yield


def _prob_pass(s_ref, m_ref, v_views, tk, result):
    mb = m_ref[...]
    lpart = None
    acc = None
    off = 0
    for v_ref in v_views:
        n_keys = v_ref.shape[0]
        for c0 in range(0, n_keys, tk):
            n = min(tk, n_keys - c0)
            ps = []
            for c in range(n // LANE):
                pc = jnp.exp2(s_ref[:, off + c * LANE:off + (c + 1) * LANE] - mb)
                lpart = pc if lpart is None else lpart + pc
                ps.append(pc.astype(BF16))
            pv = jnp.dot(jnp.concatenate(ps, axis=1), v_ref[c0:c0 + n, :], preferred_element_type=F32)
            acc = pv if acc is None else acc + pv
            off += n
            yield
    result.append(acc * (1.0 / jnp.sum(lpart, axis=-1, keepdims=True)))
    yield


def _interleave(*gens):
    live = list(gens)
    while live:
        for g in list(live):
            try:
                next(g)
            except StopIteration:
                live.remove(g)


def _pipelined_attention_kernel(*refs, n_src, n_extra, tk, n_units, load_q, finish):
    q_ref = refs[0]
    k_views = [_View(r, (0,)) for r in refs[1:1 + n_src]]
    v_views = [_View(r, (0,)) for r in refs[1 + n_src:1 + 2 * n_src]]
    extra = refs[1 + 2 * n_src:1 + 2 * n_src + n_extra]
    o_ref, s_a, s_b, m_a, m_b = refs[1 + 2 * n_src + n_extra:]
    t = pl.program_id(0)
    bufs = ((s_a, m_a), (s_b, m_b))

    def score(slot):
        return _score_pass(load_q(q_ref), k_views, bufs[slot][0], bufs[slot][1], tk)

    def prob(slot):
        result = []
        yield from _prob_pass(bufs[slot][0], bufs[slot][1], v_views, tk, result)
        finish(result[0], o_ref, extra)

    @pl.when(t == 0)
    def _():
        _interleave(score(0))

    for parity in range(2):
        @pl.when((t > 0) & (t < n_units) & (t % 2 == parity))
        def _():
            _interleave(score(parity), prob(1 - parity))

    @pl.when(t == n_units)
    def _():
        _interleave(prob((n_units - 1) % 2))


def _pipelined_attention(q, q_width, k_list, v_list, extra, heads, out_width, tq, tk, rows_per_q, load_q, finish, name):
    b, t_q, _ = q.shape
    nq = t_q // tq
    n_units = b * heads * nq

    def unit(t):
        u = jnp.minimum(t, n_units - 1)
        return u // (heads * nq), (u // nq) % heads, u % nq

    def q_map(t):
        bi, h, i = unit(t)
        return bi, i, h

    def o_map(t):
        bi, h, i = unit(jnp.maximum(t - 1, 0))
        return bi, i, h

    def kv_map(col0, lag):
        def f(t):
            bi, h, _ = unit(jnp.maximum(t - lag, 0))
            return bi, 0, col0 + h
        return f

    ins = [q]
    in_specs = [pl.BlockSpec((1, tq, q_width), q_map)]
    total = 0
    for arr, width, col0 in k_list:
        total += arr.shape[1]
        ins.append(arr)
        in_specs.append(pl.BlockSpec((1, arr.shape[1], width), kv_map(col0, 0)))
    for arr, width, col0 in v_list:
        ins.append(arr)
        in_specs.append(pl.BlockSpec((1, arr.shape[1], width), kv_map(col0, 1)))
    ins += list(extra)
    in_specs += [pl.BlockSpec(a.shape, lambda t: (0, 0)) for a in extra]
    m_rows = rows_per_q * tq
    return pl.pallas_call(
        functools.partial(_pipelined_attention_kernel, n_src=len(k_list), n_extra=len(extra), tk=tk,
                          n_units=n_units, load_q=load_q, finish=finish),
        grid=(n_units + 1,),
        in_specs=in_specs,
        out_specs=pl.BlockSpec((1, tq, out_width), o_map),
        out_shape=jax.ShapeDtypeStruct((b, t_q, heads * out_width), BF16),
        scratch_shapes=[pltpu.VMEM((m_rows, total), F32), pltpu.VMEM((m_rows, total), F32),
                        pltpu.VMEM((m_rows, LANE), F32), pltpu.VMEM((m_rows, LANE), F32)],
        compiler_params=_cparams(("arbitrary",)),
        name=name,
    )(*ins)


def _mla_load_q(q_ref):
    return q_ref[0]


def _mla_finish(o, o_ref, extra):
    o_ref[0] = o.astype(BF16)


def _mla_attention(q, kv_list, tq, tk):
    return _pipelined_attention(
        q, MLA_QK_PAD, [(k, MLA_QK_PAD, 0) for k, _ in kv_list], [(v, MLA_V, 0) for _, v in kv_list], (),
        MLA_HEADS, MLA_V, tq, tk, 1, _mla_load_q, _mla_finish, "mla_attention")


def _diff_load_q(q_ref):
    q = q_ref[0]
    first = lax.broadcasted_iota(jnp.int32, q.shape, 1) < DIFF_QK_DIM
    zero = jnp.zeros_like(q)
    return jnp.concatenate([jnp.where(first, q, zero), jnp.where(first, zero, q)], axis=0)


def _diff_finish(o, o_ref, extra, lam_init):
    lq1_ref, lk1_ref, lq2_ref, lk2_ref, og_ref = extra
    tq = o.shape[0] // 2
    lam = (jnp.exp(jnp.sum(lq1_ref[...] * lk1_ref[...], axis=-1, keepdims=True))
           - jnp.exp(jnp.sum(lq2_ref[...] * lk2_ref[...], axis=-1, keepdims=True)) + lam_init)
    d = o[:tq] - lam * o[tq:]
    r = lax.rsqrt(jnp.mean(d * d, axis=-1, keepdims=True) + EPS)
    o_ref[0] = (((d * r) * og_ref[...]) * (1.0 - lam_init)).astype(BF16)


def _diff_attention(q, kv_list, v_col0, lams, out_gain, lam_init, tq, tk):
    extra = [a.reshape(1, -1) for a in lams] + [out_gain.reshape(1, -1)]
    return _pipelined_attention(
        q, LANE, [(k, LANE, 0) for k, _ in kv_list], [(pv, LANE, v_col0) for _, pv in kv_list], extra,
        DIFF_HEADS, DIFF_V_DIM, tq, tk, 2, _diff_load_q, functools.partial(_diff_finish, lam_init=lam_init),
        "diff_attention")


def _stack_heads(q_ref):
    return jnp.concatenate([q_ref[0, :, g * LANE:(g + 1) * LANE] for g in range(SWA_GROUP)], axis=0)


def _sink_rows(sink_ref, h, tq):
    return jnp.concatenate(
        [jnp.full((tq, 1), sink_ref[h * SWA_GROUP + g] * LOG2E, F32) for g in range(SWA_GROUP)], axis=0)


def _swa_ctx_kernel(sink_ref, q_ref, k_ref, v_ref, o_ref, s_scr):
    tq = q_ref.shape[1]
    o = _dense_attention(_stack_heads(q_ref), [(_View(k_ref, (0,)), _View(v_ref, (0,)))], s_scr, k_ref.shape[1],
                         sink=_sink_rows(sink_ref, pl.program_id(1), tq))
    for g in range(SWA_GROUP):
        o_ref[0, :, g * LANE:(g + 1) * LANE] = o[g * tq:(g + 1) * tq].astype(BF16)


def _swa_window_kernel(sink_ref, q_ref, k_ref, v_ref, kc_ref, vc_ref, o_ref, *, sub):
    i = pl.program_id(1)
    tq = q_ref.shape[1]
    s_len = k_ref.shape[1]
    win = sub + 2 * SWA_WINDOW
    dn = (((1,), (1,)), ((), ()))
    for j in range(tq // sub):
        q0 = i * tq + j * sub
        start = pl.multiple_of(jnp.clip(q0 - SWA_WINDOW, 0, s_len - win), LANE)
        rel = (lax.broadcasted_iota(jnp.int32, (sub, win), 1) - lax.broadcasted_iota(jnp.int32, (sub, win), 0)
               + (start - q0))
        bias1 = jnp.where(jnp.abs(rel) <= SWA_WINDOW, 0.0, NEG_BIG).astype(F32)
        bias = jnp.concatenate([bias1] * SWA_GROUP, axis=0)
        for h in range(SWA_KV_HEADS):
            qq = jnp.concatenate(
                [q_ref[0, j * sub:(j + 1) * sub, (h * SWA_GROUP + g) * LANE:(h * SWA_GROUP + g + 1) * LANE]
                 for g in range(SWA_GROUP)], axis=0)
            hl = slice(h * LANE, (h + 1) * LANE)
            s_w = lax.dot_general(qq, k_ref[0, pl.ds(start, win), hl], dn, preferred_element_type=F32) + bias
            s_c = lax.dot_general(qq, kc_ref[0, :, hl], dn, preferred_element_type=F32)
            sink = _sink_rows(sink_ref, h, sub)
            mf = jnp.maximum(_fold_lanes(s_w, jnp.maximum), _fold_lanes(s_c, jnp.maximum))
            m = jnp.maximum(jnp.max(mf, axis=-1, keepdims=True), sink)
            p_w = jnp.exp2(s_w - m)
            p_c = jnp.exp2(s_c - m)
            lf = _fold_lanes(p_w, jnp.add) + _fold_lanes(p_c, jnp.add)
            l = jnp.sum(lf, axis=-1, keepdims=True) + jnp.exp2(sink - m)
            o = (jnp.dot(p_w.astype(BF16), v_ref[0, pl.ds(start, win), hl], preferred_element_type=F32)
                 + jnp.dot(p_c.astype(BF16), vc_ref[0, :, hl], preferred_element_type=F32)) * (1.0 / l)
            for g in range(SWA_GROUP):
                col = (h * SWA_GROUP + g) * LANE
                o_ref[0, j * sub:(j + 1) * sub, col:col + LANE] = o[g * sub:(g + 1) * sub].astype(BF16)


def _swa_attention(q, k, pv, kc, pvc, sink, tq, sub):
    b, s_len, _ = q.shape
    lc = kc.shape[1]
    kvw = SWA_KV_HEADS * SWA_HEAD_DIM
    vblk = C_SWA_V // kvw
    assert C_SWA_V % kvw == 0
    qspec = pl.BlockSpec((1, tq, SWA_WIDTH), lambda bi, i: (bi, i, 0))
    return pl.pallas_call(
        functools.partial(_swa_window_kernel, sub=sub),
        grid=(b, s_len // tq),
        in_specs=[
            pl.BlockSpec(memory_space=pltpu.SMEM),
            qspec,
            pl.BlockSpec((1, s_len, kvw), lambda bi, i: (bi, 0, 0)),
            pl.BlockSpec((1, s_len, kvw), lambda bi, i: (bi, 0, vblk)),
            pl.BlockSpec((1, lc, kvw), lambda bi, i: (bi, 0, 0)),
            pl.BlockSpec((1, lc, kvw), lambda bi, i: (bi, 0, vblk)),
        ],
        out_specs=qspec,
        out_shape=jax.ShapeDtypeStruct((b, s_len, SWA_WIDTH), BF16),
        compiler_params=_cparams(("parallel", "arbitrary")),
        name="swa_window_attention",
    )(sink, q, k, pv, kc, pvc)


def _swa_ctx_attention(q, k, pv, sink):
    b, lc, _ = q.shape
    vblk = C_SWA_V // LANE
    qspec = pl.BlockSpec((1, lc, SWA_GROUP * LANE), lambda bi, h: (bi, 0, h))
    return pl.pallas_call(
        _swa_ctx_kernel,
        grid=(b, SWA_KV_HEADS),
        in_specs=[
            pl.BlockSpec(memory_space=pltpu.SMEM),
            qspec,
            pl.BlockSpec((1, lc, LANE), lambda bi, h: (bi, 0, h)),
            pl.BlockSpec((1, lc, LANE), lambda bi, h: (bi, 0, vblk + h)),
        ],
        out_specs=qspec,
        out_shape=jax.ShapeDtypeStruct((b, lc, SWA_WIDTH), BF16),
        scratch_shapes=[pltpu.VMEM((SWA_GROUP * lc, lc), F32)],
        compiler_params=_cparams(("parallel", "parallel")),
        name="swa_ctx_attention",
    )(sink, q, k, pv)


def _outproj_kernel(ya_ref, yb_ref, yc_ref, gate_ref, h_ref, g_ref, w_ref, o_ref):
    y = jnp.concatenate([ya_ref[0], yb_ref[0], yc_ref[0]], axis=-1).astype(F32)
    gt = gate_ref[0].astype(F32)
    a = (y * (gt * (1.0 / (1.0 + jnp.exp(-gt))))).astype(BF16)
    o_ref[0] = h_ref[0] + g_ref[0] * jnp.dot(a, w_ref[...], preferred_element_type=F32)


def _outproj(ya, yb, yc, p, h, g, w, tm):
    bx, tx, d = h.shape
    tok = lambda wd: pl.BlockSpec((1, tm, wd), lambda b, i: (b, i, 0))
    return pl.pallas_call(
        _outproj_kernel,
        grid=(bx, tx // tm),
        in_specs=[
            tok(MLA_WIDTH), tok(SWA_WIDTH), tok(DIFF_WIDTH),
            pl.BlockSpec((1, tm, MIX_WIDTH), lambda b, i: (b, i, C_GATE // MIX_WIDTH)),
            tok(d),
            pl.BlockSpec((1, 1, d), lambda b, i: (b, 0, 0)),
            pl.BlockSpec(w.shape, lambda b, i: (0, 0)),
        ],
        out_specs=tok(d),
        out_shape=jax.ShapeDtypeStruct((bx, tx, d), F32),
        compiler_params=_cparams(("parallel", "parallel")),
        name="gated_outproj_residual",
    )(ya, yb, yc, p, h, g, w)


def _layer_weights(l, w_in, mla_w_uq, mla_w_ukv, mla_q_gain, mla_k_gain, w_out):
    d = w_in.shape[1]
    wi = w_in[l]
    o = np.cumsum([0, MLA_Q_RANK, MLA_KV_RANK, MLA_ROPE, MLA_WIDTH, SWA_WIDTH, SWA_KV_HEADS * SWA_HEAD_DIM,
                   SWA_KV_HEADS * SWA_HEAD_DIM, SWA_WIDTH, DIFF_WIDTH, DIFF_WIDTH, DIFF_WIDTH, DIFF_WIDTH]).tolist()
    seg = lambda j: wi[:, o[j]:o[j + 1]]
    z = lambda n: jnp.zeros((d, n), wi.dtype)
    w_p = jnp.concatenate([seg(0), seg(1), seg(2), z(C_SWA_Q - C_KR - MLA_ROPE),
                           seg(4), seg(5), seg(6), seg(8), seg(9), seg(10), z(C_QKV_END - C_DIF_V - DIFF_WIDTH),
                           seg(3), seg(7), seg(11)], axis=1).astype(BF16)
    pad = MLA_QK_PAD - MLA_QK
    w_uq = jnp.pad(mla_w_uq[l].reshape(MLA_Q_RANK, MLA_HEADS, MLA_QK), ((0, 0), (0, 0), (0, pad)))
    w_ukv = mla_w_ukv[l].reshape(MLA_KV_RANK, MLA_HEADS, MLA_NOPE + MLA_V)
    return {
        "w_p": w_p,
        "w_uq": w_uq.reshape(MLA_Q_RANK, MLA_HEADS * MLA_QK_PAD).astype(BF16),
        "w_kn": w_ukv[:, :, :MLA_NOPE].reshape(MLA_KV_RANK, MLA_HEADS * MLA_NOPE).astype(BF16),
        "w_v": w_ukv[:, :, MLA_NOPE:].reshape(MLA_KV_RANK, MLA_WIDTH).astype(BF16),
        "q_gain": jnp.pad(mla_q_gain[l], (0, pad)),
        "k_gain_nope": mla_k_gain[l][:MLA_NOPE],
        "k_gain_rope": jnp.pad(mla_k_gain[l][MLA_NOPE:], (0, LANE - MLA_ROPE)),
        "w_out": w_out[l].astype(BF16),
    }


def _rope_tables(n_tokens):
    t = np.arange(n_tokens)
    row, col = (t // GRID_W).astype(np.float64), (t % GRID_W).astype(np.float64)

    def cos_sin(rot_dim):
        n_freq = rot_dim // 4
        inv = np.power(ROPE_BASE, -np.arange(n_freq, dtype=np.float32) / n_freq).astype(np.float32)
        ang = np.concatenate([row[:, None].astype(np.float32) * inv, col[:, None].astype(np.float32) * inv], axis=-1)
        return np.cos(ang).astype(np.float32), np.sin(ang).astype(np.float32)

    c64, s64 = cos_sin(MLA_ROPE)
    c128, s128 = cos_sin(SWA_HEAD_DIM)
    zeros = np.zeros_like(s64)
    cd = np.concatenate([c64] * 4, axis=-1)
    sneg = np.concatenate([-s64, zeros, -s64, zeros], axis=-1)
    spos = np.concatenate([zeros, s64, zeros, s64], axis=-1)
    cs = np.concatenate([c128, c128], axis=-1)
    ss = np.concatenate([-s128, s128], axis=-1)
    return tuple(jnp.asarray(a, F32) for a in (cd, sneg, spos, cs, ss))


def _block_diag_ones(n, seg):
    idx = np.arange(n) // seg
    return jnp.asarray(idx[:, None] == idx[None, :], BF16)


def _pick(n, prefs):
    for t in prefs:
        if n % t == 0:
            return t
    return n


def kernel(x, c, ctx, c_ctx, norm_g, w_ada, b_ada, w_in, mla_q_norm, mla_w_uq, mla_kv_norm, mla_w_ukv,
           mla_q_gain, mla_k_gain, swa_q_gain, swa_k_gain, swa_sink, dif_q_gain, dif_k_gain,
           dif_lq1, dif_lk1, dif_lq2, dif_lk2, dif_out_gain, w_out):
    b, s, d = x.shape
    lc = ctx.shape[1]
    depth = w_in.shape[0]
    assert s % (2 * LANE) == 0 and s >= 4 * LANE and lc % LANE == 0 and d == MIX_WIDTH

    tables = _rope_tables(s)
    bd = _block_diag_ones(DIFF_WIDTH, DIFF_QK_DIM)

    n_rows = -(-(b + 1) // 8) * 8
    cvec = jnp.concatenate([c, c_ctx[None, :], jnp.zeros((n_rows - b - 1, d), F32)], axis=0)
    mod = _modulation(cvec, w_ada, b_ada)

    hx = x
    hc = ctx.reshape(1, b * lc, d)
    tm_x = _pick(s, (1024, 512, 256))
    tm_c = _pick(b * lc, (1024, 512, 256))
    tq_a = _pick(s, (1024, 512, 256))
    tq_b = _pick(s, (512, 256))
    tq_c = _pick(s, (512, 256))
    tk = 512
    tn = 1536

    for l in range(depth):
        need_ctx_out = l < depth - 1
        wl = _layer_weights(l, w_in, mla_w_uq, mla_w_ukv, mla_q_gain, mla_k_gain, w_out)
        wl.update(mla_q_norm=mla_q_norm[l], mla_kv_norm=mla_kv_norm[l], swa_q_gain=swa_q_gain[l],
                  swa_k_gain=swa_k_gain[l], dif_q_gain=jnp.tile(dif_q_gain[l], DIFF_WIDTH // DIFF_QK_DIM),
                  dif_k_gain=jnp.tile(dif_k_gain[l], DIFF_WIDTH // DIFF_QK_DIM), bd=bd)
        m = mod[l]
        sh_x, sc_x, g_x = (m[:b, j * d:(j + 1) * d].reshape(b, 1, d) for j in range(3))
        sh_c, sc_c, g_c = (m[b:b + 1, j * d:(j + 1) * d].reshape(1, 1, d) for j in range(3))
        ng = norm_g[l].reshape(1, d)
        lam_init = 0.8 - 0.6 * math.exp(-0.3 * l)
        lams = (dif_lq1[l], dif_lk1[l], dif_lq2[l], dif_lk2[l])
        sink = swa_sink[l]

        px = _inproj(hx, sc_x, sh_x, ng, wl["w_p"], tm_x, tn)
        pc = _inproj(hc, sc_c, sh_c, ng, wl["w_p"], tm_c, tn)
        qa_x, ka_x, va_x, qb_x, kb_x, qc_x, kc_x = _prep(px, tables, wl, _pick(s, (512, 256)))
        qa_c, ka_c, va_c, qb_c, kb_c, qc_c, kc_c = (
            t.reshape(b, lc, -1) for t in _prep(pc, None, wl, _pick(b * lc, (512, 256))))
        pcb = pc.reshape(b, lc, P_COLS)
        vblk_c = C_DIF_V // LANE

        ya_x = _mla_attention(qa_x, [(ka_x, va_x), (ka_c, va_c)], tq_a, tk)
        yb_x = _swa_attention(qb_x, kb_x, px, kb_c, pcb, sink, tq_b, 256)
        yc_x = _diff_attention(qc_x, [(kc_x, px), (kc_c, pcb)], vblk_c, lams, dif_out_gain[l], lam_init, tq_c, tk)

        if need_ctx_out:
            ya_c = _mla_attention(qa_c, [(ka_c, va_c)], lc, tk)
            yb_c = _swa_ctx_attention(qb_c, kb_c, pcb, sink)
            yc_c = _diff_attention(qc_c, [(kc_c, pcb)], vblk_c, lams, dif_out_gain[l], lam_init, lc, tk)
            flat = lambda t: t.reshape(1, b * lc, -1)
            hc = _outproj(flat(ya_c), flat(yb_c), flat(yc_c), pc, hc, g_c, wl["w_out"], _pick(b * lc, (512, 256)))

        hx = _outproj(ya_x, yb_x, yc_x, px, hx, g_x, wl["w_out"], _pick(s, (512, 256)))

    return hx
```

```python
import functools
import math

import numpy as np
import jax
import jax.numpy as jnp
from jax import lax
from jax.experimental import pallas as pl
from jax.experimental.pallas import tpu as pltpu

F32 = jnp.float32
BF16 = jnp.bfloat16

GRID_W = 64
ROPE_BASE = 10000.0
EPS = 1e-6
LANE = 128

MLA_HEADS = 6
MLA_Q_RANK = 512
MLA_KV_RANK = 256
MLA_NOPE = 128
MLA_ROPE = 64
MLA_V = 128
MLA_QK = MLA_NOPE + MLA_ROPE
MLA_QK_PAD = 2 * LANE
MLA_WIDTH = MLA_HEADS * MLA_V

SWA_Q_HEADS = 6
SWA_KV_HEADS = 2
SWA_GROUP = SWA_Q_HEADS // SWA_KV_HEADS
SWA_HEAD_DIM = 128
SWA_WINDOW = 128
SWA_WIDTH = SWA_Q_HEADS * SWA_HEAD_DIM
SWA_KV_WIDTH = SWA_KV_HEADS * SWA_HEAD_DIM

DIFF_HEADS = 4
DIFF_QK_DIM = 64
DIFF_V_DIM = 2 * DIFF_QK_DIM
DIFF_WIDTH = DIFF_HEADS * DIFF_V_DIM

MIX_WIDTH = MLA_WIDTH + SWA_WIDTH + DIFF_WIDTH

LOG2E = math.log2(math.e)
NEG_BIG = -1e30

P_TILE = 1536
N_P_TILES = 4
T0_CQ = 0
T0_CKV = T0_CQ + MLA_Q_RANK
T0_KR = T0_CKV + MLA_KV_RANK
T0_SWA_K = T0_KR + LANE
T0_SWA_V = T0_SWA_K + SWA_KV_WIDTH
T1_SWA_Q = 0
T1_DIF_Q = T1_SWA_Q + SWA_WIDTH
T2_DIF_K = 0
T2_DIF_V = T2_DIF_K + DIFF_WIDTH
T2_DIF_G = T2_DIF_V + DIFF_WIDTH
GATE_AB = MLA_WIDTH + SWA_WIDTH
assert T0_SWA_V + SWA_KV_WIDTH <= P_TILE and T1_DIF_Q + DIFF_WIDTH <= P_TILE
assert T2_DIF_G + DIFF_WIDTH == P_TILE and GATE_AB == P_TILE

VMEM_LIMIT = 60 * 1024 * 1024


def _cparams(sem):
    return pltpu.CompilerParams(dimension_semantics=sem, vmem_limit_bytes=VMEM_LIMIT)


def _mod_kernel(c_ref, w_ref, b_ref, o_ref):
    c = c_ref[...]
    a = c * (1.0 / (1.0 + jnp.exp(-c)))
    o_ref[0] = jnp.dot(a, w_ref[0], preferred_element_type=F32, precision=lax.Precision.HIGHEST) + b_ref[0]


def _modulation(cvec, w_ada, b_ada):
    depth, d, n3 = w_ada.shape
    rows = cvec.shape[0]
    tn = 1536
    return pl.pallas_call(
        _mod_kernel,
        grid=(depth, n3 // tn),
        in_specs=[
            pl.BlockSpec((rows, d), lambda l, n: (0, 0)),
            pl.BlockSpec((1, d, tn), lambda l, n: (l, 0, n)),
            pl.BlockSpec((1, 1, tn), lambda l, n: (l, 0, n)),
        ],
        out_specs=pl.BlockSpec((1, rows, tn), lambda l, n: (l, 0, n)),
        out_shape=jax.ShapeDtypeStruct((depth, rows, n3), F32),
        compiler_params=_cparams(("parallel", "parallel")),
        name="adaln_modulation",
    )(cvec, w_ada, b_ada.reshape(depth, 1, n3))


def _rope64(x, cd, sneg, spos):
    return x * cd + pltpu.roll(x, 96, 1) * sneg + pltpu.roll(x, 32, 1) * spos


def _rope128(x, cs, ss):
    return x * cs + pltpu.roll(x, 64, 1) * ss


def _proj_prep_kernel(*refs, rope):
    it = iter(refs)
    take = lambda n: [next(it) for _ in range(n)]
    x_ref, sc_ref, sh_ref, g_ref, w_ref = take(5)
    if rope:
        cd_ref, sneg_ref, spos_ref, cs_ref, ss_ref = take(5)
        rope64 = lambda t: _rope64(t, cd_ref[...], sneg_ref[...], spos_ref[...])
        rope128 = lambda t: _rope128(t, cs_ref[...], ss_ref[...])
    else:
        rope64 = rope128 = lambda t: t
    (qn_g_ref, wuq_ref, qgain_ref, kvn_g_ref, wkn_ref, wv_ref, kgn_ref, kgr_ref,
     sqg_ref, skg_ref, dqg_ref, dkg_ref, bd_ref) = take(13)
    qa_ref, ka_ref, va_ref, kb_ref, vb_ref, qb_ref, qc_ref, kc_ref, vc_ref, gc_ref, gab_ref = take(11)
    xn_ref, acc_a, acc_b = take(3)
    accs = (acc_a, acc_b)
    j = pl.program_id(2)

    def rms(t, width):
        return lax.rsqrt(jnp.sum(t * t, axis=-1, keepdims=True) * (1.0 / width) + EPS)

    def seg_norm(t, gain):
        t2 = t * t
        hi = t2.astype(BF16)
        lo = (t2 - hi.astype(F32)).astype(BF16)
        ssq = (jnp.dot(hi, bd_ref[...], preferred_element_type=F32)
               + jnp.dot(lo, bd_ref[...], preferred_element_type=F32))
        return (t * lax.rsqrt(ssq * (1.0 / DIFF_QK_DIM) + EPS)) * gain

    def prep_tile0(acc):
        cq = acc[:, T0_CQ:T0_CQ + MLA_Q_RANK]
        cqn = ((cq * rms(cq, MLA_Q_RANK)) * qn_g_ref[...]).astype(BF16)
        q = jnp.dot(cqn, wuq_ref[...], preferred_element_type=F32)
        qscale = MLA_QK ** -0.5 * LOG2E
        for h in range(MLA_HEADS):
            qh = q[:, h * MLA_QK_PAD:(h + 1) * MLA_QK_PAD]
            qh = (qh * rms(qh, MLA_QK)) * qgain_ref[...]
            qa_ref[0, :, h * MLA_QK_PAD:h * MLA_QK_PAD + LANE] = (qh[:, :LANE] * qscale).astype(BF16)
            qa_ref[0, :, h * MLA_QK_PAD + LANE:(h + 1) * MLA_QK_PAD] = (rope64(qh[:, LANE:]) * qscale).astype(BF16)
        ckv = acc[:, T0_CKV:T0_CKV + MLA_KV_RANK]
        ckvn = ((ckv * rms(ckv, MLA_KV_RANK)) * kvn_g_ref[...]).astype(BF16)
        kn = jnp.dot(ckvn, wkn_ref[...], preferred_element_type=F32)
        va_ref[0] = jnp.dot(ckvn, wv_ref[...], preferred_element_type=F32).astype(BF16)
        krp = acc[:, T0_KR:T0_KR + LANE]
        kr_ssq = jnp.sum(krp * krp, axis=-1, keepdims=True)
        kr_rot = rope64(krp * kgr_ref[...])
        for h in range(MLA_HEADS):
            knh = kn[:, h * LANE:(h + 1) * LANE]
            r = lax.rsqrt((jnp.sum(knh * knh, axis=-1, keepdims=True) + kr_ssq) * (1.0 / MLA_QK) + EPS)
            ka_ref[0, :, h * MLA_QK_PAD:h * MLA_QK_PAD + LANE] = ((knh * r) * kgn_ref[...]).astype(BF16)
            ka_ref[0, :, h * MLA_QK_PAD + LANE:(h + 1) * MLA_QK_PAD] = (kr_rot * r).astype(BF16)
        for h in range(SWA_KV_HEADS):
            t = acc[:, T0_SWA_K + h * LANE:T0_SWA_K + (h + 1) * LANE]
            kb_ref[0, :, h * LANE:(h + 1) * LANE] = rope128((t * rms(t, SWA_HEAD_DIM)) * skg_ref[...]).astype(BF16)
        vb_ref[0] = acc[:, T0_SWA_V:T0_SWA_V + SWA_KV_WIDTH].astype(BF16)

    def prep_tile1(acc):
        bscale = SWA_HEAD_DIM ** -0.5 * LOG2E
        for h in range(SWA_Q_HEADS):
            t = acc[:, T1_SWA_Q + h * LANE:T1_SWA_Q + (h + 1) * LANE]
            t = rope128((t * rms(t, SWA_HEAD_DIM)) * sqg_ref[...])
            qb_ref[0, :, h * LANE:(h + 1) * LANE] = (t * bscale).astype(BF16)
        cscale = DIFF_QK_DIM ** -0.5 * LOG2E
        tq = seg_norm(acc[:, T1_DIF_Q:T1_DIF_Q + DIFF_WIDTH], dqg_ref[...])
        for c in range(DIFF_WIDTH // LANE):
            sl = slice(c * LANE, (c + 1) * LANE)
            qc_ref[0, :, sl] = (rope64(tq[:, sl]) * cscale).astype(BF16)

    def prep_tile2(acc):
        tk = seg_norm(acc[:, T2_DIF_K:T2_DIF_K + DIFF_WIDTH], dkg_ref[...])
        for c in range(DIFF_WIDTH // LANE):
            sl = slice(c * LANE, (c + 1) * LANE)
            kc_ref[0, :, sl] = rope64(tk[:, sl]).astype(BF16)
        vc_ref[0] = acc[:, T2_DIF_V:T2_DIF_V + DIFF_WIDTH].astype(BF16)
        gc_ref[0] = acc[:, T2_DIF_G:T2_DIF_G + DIFF_WIDTH].astype(BF16)

    def prep_tile3(acc):
        gab_ref[0] = acc[...].astype(BF16)

    preps = (prep_tile0, prep_tile1, prep_tile2, prep_tile3)

    @pl.when(j == 0)
    def _():
        x = x_ref[0]
        r = lax.rsqrt(jnp.mean(x * x, axis=-1, keepdims=True) + EPS)
        y = (x * r) * g_ref[...]
        xn_ref[...] = (y * (1.0 + sc_ref[0]) + sh_ref[0]).astype(BF16)

    for step in range(N_P_TILES + 1):
        @pl.when(j == step)
        def _():
            if step > 0:
                preps[step - 1](accs[(step - 1) % 2])
            if step < N_P_TILES:
                accs[step % 2][...] = jnp.dot(xn_ref[...], w_ref[...], preferred_element_type=F32)


_PREP_OUT_WIDTHS = (MLA_HEADS * MLA_QK_PAD, MLA_HEADS * MLA_QK_PAD, MLA_WIDTH, SWA_KV_WIDTH, SWA_KV_WIDTH,
                    SWA_WIDTH, DIFF_WIDTH, DIFF_WIDTH, DIFF_WIDTH, DIFF_WIDTH, GATE_AB)


def _proj_prep(h, sc, sh, norm_g, tables, wl, tm):
    bx, tx, d = h.shape
    rope = tables is not None
    row = lambda a: a.reshape(1, -1)
    full = lambda a: pl.BlockSpec(a.shape, lambda b, i, j: (0,) * a.ndim)
    consts = [row(wl["mla_q_norm"]), wl["w_uq"], row(wl["q_gain"]), row(wl["mla_kv_norm"]), wl["w_kn"], wl["w_v"],
              row(wl["k_gain_nope"]), row(wl["k_gain_rope"]), row(wl["swa_q_gain"]), row(wl["swa_k_gain"]),
              row(wl["dif_q_gain"]), row(wl["dif_k_gain"]), wl["bd"]]
    ins = [h, sc, sh, norm_g, wl["w_p"]]
    in_specs = [
        pl.BlockSpec((1, tm, d), lambda b, i, j: (b, i, 0)),
        pl.BlockSpec((1, 1, d), lambda b, i, j: (b, 0, 0)),
        pl.BlockSpec((1, 1, d), lambda b, i, j: (b, 0, 0)),
        pl.BlockSpec((1, d), lambda b, i, j: (0, 0)),
        pl.BlockSpec((d, P_TILE), lambda b, i, j: (0, jnp.minimum(j, N_P_TILES - 1))),
    ]
    if rope:
        ins += list(tables)
        in_specs += [pl.BlockSpec((tm, LANE), lambda b, i, j: (i, 0))] * len(tables)
    ins += consts
    in_specs += [full(a) for a in consts]
    return pl.pallas_call(
        functools.partial(_proj_prep_kernel, rope=rope),
        grid=(bx, tx // tm, N_P_TILES + 1),
        in_specs=in_specs,
        out_specs=[pl.BlockSpec((1, tm, w), lambda b, i, j: (b, i, 0)) for w in _PREP_OUT_WIDTHS],
        out_shape=[jax.ShapeDtypeStruct((bx, tx, w), BF16) for w in _PREP_OUT_WIDTHS],
        scratch_shapes=[pltpu.VMEM((tm, d), BF16), pltpu.VMEM((tm, P_TILE), F32), pltpu.VMEM((tm, P_TILE), F32)],
        compiler_params=_cparams(("parallel", "parallel", "arbitrary")),
        name="proj_prep_rope" if rope else "proj_prep",
    )(*ins)


def _fold_lanes(t, op):
    parts = [t[:, c * LANE:(c + 1) * LANE] for c in range(t.shape[1] // LANE)]
    return functools.reduce(op, parts)


def _key_chunks(srcs, tk):
    out = []
    for k_ref, v_ref in srcs:
        n = k_ref.shape[0]
        for c0 in range(0, n, tk):
            out.append((k_ref, v_ref, c0, min(tk, n - c0)))
    return out


def _dense_attention(q, srcs, s_scr, tk, sink=None):
    chunks = _key_chunks(srcs, tk)
    mpart = None
    off = 0
    for k_ref, _, c0, n in chunks:
        s = lax.dot_general(q, k_ref[c0:c0 + n, :], (((1,), (1,)), ((), ())), preferred_element_type=F32)
        s_scr[:, off:off + n] = s
        f = _fold_lanes(s, jnp.maximum)
        mpart = f if mpart is None else jnp.maximum(mpart, f)
        off += n
    m = jnp.max(mpart, axis=-1, keepdims=True)
    if sink is not None:
        m = jnp.maximum(m, sink)
    lpart = None
    acc = None
    off = 0
    for _, v_ref, c0, n in chunks:
        p = jnp.exp2(s_scr[:, off:off + n] - m)
        f = _fold_lanes(p, jnp.add)
        lpart = f if lpart is None else lpart + f
        pv = jnp.dot(p.astype(BF16), v_ref[c0:c0 + n, :], preferred_element_type=F32)
        acc = pv if acc is None else acc + pv
        off += n
    l = jnp.sum(lpart, axis=-1, keepdims=True)
    if sink is not None:
        l = l + jnp.exp2(sink - m)
    return acc * (1.0 / l)


class _View:
    def __init__(self, ref, lead):
        self.ref, self.lead = ref, lead
        self.shape = ref.shape[len(lead):]

    def __getitem__(self, idx):
        return self.ref[self.lead + idx]


def _score_pass(q, k_views, s_ref, m_ref, tk):
    mpart = None
    off = 0
    for k_ref in k_views:
        n_keys = k_ref.shape[0]
        for c0 in range(0, n_keys, tk):
            n = min(tk, n_keys - c0)
            s = lax.dot_general(q, k_ref[c0:c0 + n, :], (((1,), (1,)), ((), ())), preferred_element_type=F32)
            s_ref[:, off:off + n] = s
            f = _fold_lanes(s, jnp.maximum)
            mpart = f if mpart is None else jnp.maximum(mpart, f)
            off += n
            yield
    m_ref[...] = jnp.broadcast_to(jnp.max(mpart, axis=-1, keepdims=True), m_ref.shape)
    yield


def _prob_pass(s_ref, m_ref, v_views, tk, result):
    mb = m_ref[...]
    lpart = None
    acc = None
    off = 0
    for v_ref in v_views:
        n_keys = v_ref.shape[0]
        for c0 in range(0, n_keys, tk):
            n = min(tk, n_keys - c0)
            ps = []
            for c in range(n // LANE):
                pc = jnp.exp2(s_ref[:, off + c * LANE:off + (c + 1) * LANE] - mb)
                lpart = pc if lpart is None else lpart + pc
                ps.append(pc.astype(BF16))
            pv = jnp.dot(jnp.concatenate(ps, axis=1), v_ref[c0:c0 + n, :], preferred_element_type=F32)
            acc = pv if acc is None else acc + pv
            off += n
            yield
    result.append(acc * (1.0 / jnp.sum(lpart, axis=-1, keepdims=True)))
    yield


def _interleave(*gens):
    live = list(gens)
    while live:
        for g in list(live):
            try:
                next(g)
            except StopIteration:
                live.remove(g)


def _pipelined_attention_kernel(*refs, n_src, n_extra, tk, n_units, load_q, finish):
    q_ref = refs[0]
    k_views = [_View(r, (0,)) for r in refs[1:1 + n_src]]
    v_views = [_View(r, (0,)) for r in refs[1 + n_src:1 + 2 * n_src]]
    extra = refs[1 + 2 * n_src:1 + 2 * n_src + n_extra]
    o_ref, s_a, s_b, m_a, m_b = refs[1 + 2 * n_src + n_extra:]
    t = pl.program_id(0)
    bufs = ((s_a, m_a), (s_b, m_b))

    def score(slot):
        return _score_pass(load_q(q_ref), k_views, bufs[slot][0], bufs[slot][1], tk)

    def prob(slot):
        result = []
        yield from _prob_pass(bufs[slot][0], bufs[slot][1], v_views, tk, result)
        finish(result[0], o_ref, extra)

    @pl.when(t == 0)
    def _():
        _interleave(score(0))

    for parity in range(2):
        @pl.when((t > 0) & (t < n_units) & (t % 2 == parity))
        def _():
            _interleave(score(parity), prob(1 - parity))

    @pl.when(t == n_units)
    def _():
        _interleave(prob((n_units - 1) % 2))


def _pipelined_attention(q, q_width, k_list, v_list, extra, heads, out_width, tq, tk, rows_per_q, load_q, finish, name):
    b, t_q, _ = q.shape
    nq = t_q // tq
    n_units = b * heads * nq

    def unit(t):
        u = jnp.minimum(t, n_units - 1)
        return u // (heads * nq), (u // nq) % heads, u % nq

    def q_map(t):
        bi, h, i = unit(t)
        return bi, i, h

    def o_map(t):
        bi, h, i = unit(jnp.maximum(t - 1, 0))
        return bi, i, h

    def kv_map(lag):
        def f(t):
            bi, h, _ = unit(jnp.maximum(t - lag, 0))
            return bi, 0, h
        return f

    ins = [q]
    in_specs = [pl.BlockSpec((1, tq, q_width), q_map)]
    total = 0
    for arr, width in k_list:
        total += arr.shape[1]
        ins.append(arr)
        in_specs.append(pl.BlockSpec((1, arr.shape[1], width), kv_map(0)))
    for arr, width in v_list:
        ins.append(arr)
        in_specs.append(pl.BlockSpec((1, arr.shape[1], width), kv_map(1)))
    ins += list(extra)
    in_specs += [pl.BlockSpec(a.shape, lambda t: (0, 0)) for a in extra]
    m_rows = rows_per_q * tq
    return pl.pallas_call(
        functools.partial(_pipelined_attention_kernel, n_src=len(k_list), n_extra=len(extra), tk=tk,
                          n_units=n_units, load_q=load_q, finish=finish),
        grid=(n_units + 1,),
        in_specs=in_specs,
        out_specs=pl.BlockSpec((1, tq, out_width), o_map),
        out_shape=jax.ShapeDtypeStruct((b, t_q, heads * out_width), BF16),
        scratch_shapes=[pltpu.VMEM((m_rows, total), F32), pltpu.VMEM((m_rows, total), F32),
                        pltpu.VMEM((m_rows, LANE), F32), pltpu.VMEM((m_rows, LANE), F32)],
        compiler_params=_cparams(("arbitrary",)),
        name=name,
    )(*ins)


def _mla_load_q(q_ref):
    return q_ref[0]


def _mla_finish(o, o_ref, extra):
    o_ref[0] = o.astype(BF16)


def _mla_attention(q, kv_list, tq, tk):
    return _pipelined_attention(
        q, MLA_QK_PAD, [(k, MLA_QK_PAD) for k, _ in kv_list], [(v, MLA_V) for _, v in kv_list], (),
        MLA_HEADS, MLA_V, tq, tk, 1, _mla_load_q, _mla_finish, "mla_attention")


def _diff_load_q(q_ref):
    q = q_ref[0]
    first = lax.broadcasted_iota(jnp.int32, q.shape, 1) < DIFF_QK_DIM
    zero = jnp.zeros_like(q)
    return jnp.concatenate([jnp.where(first, q, zero), jnp.where(first, zero, q)], axis=0)


def _diff_finish(o, o_ref, extra, lam_init):
    lq1_ref, lk1_ref, lq2_ref, lk2_ref, og_ref = extra
    tq = o.shape[0] // 2
    lam = (jnp.exp(jnp.sum(lq1_ref[...] * lk1_ref[...], axis=-1, keepdims=True))
           - jnp.exp(jnp.sum(lq2_ref[...] * lk2_ref[...], axis=-1, keepdims=True)) + lam_init)
    d = o[:tq] - lam * o[tq:]
    r = lax.rsqrt(jnp.mean(d * d, axis=-1, keepdims=True) + EPS)
    o_ref[0] = (((d * r) * og_ref[...]) * (1.0 - lam_init)).astype(BF16)


def _diff_attention(q, kv_list, lams, out_gain, lam_init, tq, tk):
    extra = [a.reshape(1, -1) for a in lams] + [out_gain.reshape(1, -1)]
    return _pipelined_attention(
        q, LANE, [(k, LANE) for k, _ in kv_list], [(v, DIFF_V_DIM) for _, v in kv_list], extra,
        DIFF_HEADS, DIFF_V_DIM, tq, tk, 2, _diff_load_q, functools.partial(_diff_finish, lam_init=lam_init),
        "diff_attention")


def _stack_heads(q_ref):
    return jnp.concatenate([q_ref[0, :, g * LANE:(g + 1) * LANE] for g in range(SWA_GROUP)], axis=0)


def _sink_rows(sink_ref, h, tq):
    return jnp.concatenate(
        [jnp.full((tq, 1), sink_ref[h * SWA_GROUP + g] * LOG2E, F32) for g in range(SWA_GROUP)], axis=0)


def _swa_ctx_kernel(sink_ref, q_ref, k_ref, v_ref, o_ref, s_scr):
    tq = q_ref.shape[1]
    o = _dense_attention(_stack_heads(q_ref), [(_View(k_ref, (0,)), _View(v_ref, (0,)))], s_scr, k_ref.shape[1],
                         sink=_sink_rows(sink_ref, pl.program_id(1), tq))
    for g in range(SWA_GROUP):
        o_ref[0, :, g * LANE:(g + 1) * LANE] = o[g * tq:(g + 1) * tq].astype(BF16)


def _swa_window_kernel(sink_ref, q_ref, k_ref, v_ref, kc_ref, vc_ref, o_ref, *, sub):
    i = pl.program_id(1)
    tq = q_ref.shape[1]
    s_len = k_ref.shape[1]
    win = sub + 2 * SWA_WINDOW
    dn = (((1,), (1,)), ((), ()))
    for j in range(tq // sub):
        q0 = i * tq + j * sub
        start = pl.multiple_of(jnp.clip(q0 - SWA_WINDOW, 0, s_len - win), LANE)
        rel = (lax.broadcasted_iota(jnp.int32, (sub, win), 1) - lax.broadcasted_iota(jnp.int32, (sub, win), 0)
               + (start - q0))
        bias1 = jnp.where(jnp.abs(rel) <= SWA_WINDOW, 0.0, NEG_BIG).astype(F32)
        bias = jnp.concatenate([bias1] * SWA_GROUP, axis=0)
        for h in range(SWA_KV_HEADS):
            qq = jnp.concatenate(
                [q_ref[0, j * sub:(j + 1) * sub, (h * SWA_GROUP + g) * LANE:(h * SWA_GROUP + g + 1) * LANE]
                 for g in range(SWA_GROUP)], axis=0)
            hl = slice(h * LANE, (h + 1) * LANE)
            s_w = lax.dot_general(qq, k_ref[0, pl.ds(start, win), hl], dn, preferred_element_type=F32) + bias
            s_c = lax.dot_general(qq, kc_ref[0, :, hl], dn, preferred_element_type=F32)
            sink = _sink_rows(sink_ref, h, sub)
            mf = jnp.maximum(_fold_lanes(s_w, jnp.maximum), _fold_lanes(s_c, jnp.maximum))
            m = jnp.maximum(jnp.max(mf, axis=-1, keepdims=True), sink)
            p_w = jnp.exp2(s_w - m)
            p_c = jnp.exp2(s_c - m)
            lf = _fold_lanes(p_w, jnp.add) + _fold_lanes(p_c, jnp.add)
            l = jnp.sum(lf, axis=-1, keepdims=True) + jnp.exp2(sink - m)
            o = (jnp.dot(p_w.astype(BF16), v_ref[0, pl.ds(start, win), hl], preferred_element_type=F32)
                 + jnp.dot(p_c.astype(BF16), vc_ref[0, :, hl], preferred_element_type=F32)) * (1.0 / l)
            for g in range(SWA_GROUP):
                col = (h * SWA_GROUP + g) * LANE
                o_ref[0, j * sub:(j + 1) * sub, col:col + LANE] = o[g * sub:(g + 1) * sub].astype(BF16)


def _swa_attention(q, k, v, kc, vc, sink, tq, sub):
    b, s_len, _ = q.shape
    lc = kc.shape[1]
    qspec = pl.BlockSpec((1, tq, SWA_WIDTH), lambda bi, i: (bi, i, 0))
    kv_lat = pl.BlockSpec((1, s_len, SWA_KV_WIDTH), lambda bi, i: (bi, 0, 0))
    kv_ctx = pl.BlockSpec((1, lc, SWA_KV_WIDTH), lambda bi, i: (bi, 0, 0))
    return pl.pallas_call(
        functools.partial(_swa_window_kernel, sub=sub),
        grid=(b, s_len // tq),
        in_specs=[pl.BlockSpec(memory_space=pltpu.SMEM), qspec, kv_lat, kv_lat, kv_ctx, kv_ctx],
        out_specs=qspec,
        out_shape=jax.ShapeDtypeStruct((b, s_len, SWA_WIDTH), BF16),
        compiler_params=_cparams(("parallel", "arbitrary")),
        name="swa_window_attention",
    )(sink, q, k, v, kc, vc)


def _swa_ctx_attention(q, k, v, sink):
    b, lc, _ = q.shape
    qspec = pl.BlockSpec((1, lc, SWA_GROUP * LANE), lambda bi, h: (bi, 0, h))
    kvspec = pl.BlockSpec((1, lc, LANE), lambda bi, h: (bi, 0, h))
    return pl.pallas_call(
        _swa_ctx_kernel,
        grid=(b, SWA_KV_HEADS),
        in_specs=[pl.BlockSpec(memory_space=pltpu.SMEM), qspec, kvspec, kvspec],
        out_specs=qspec,
        out_shape=jax.ShapeDtypeStruct((b, lc, SWA_WIDTH), BF16),
        scratch_shapes=[pltpu.VMEM((SWA_GROUP * lc, lc), F32)],
        compiler_params=_cparams(("parallel", "parallel")),
        name="swa_ctx_attention",
    )(sink, q, k, v)


def _outproj_kernel(ya_ref, yb_ref, yc_ref, gab_ref, gc_ref, h_ref, g_ref, w_ref, o_ref):
    y = jnp.concatenate([ya_ref[0], yb_ref[0], yc_ref[0]], axis=-1).astype(F32)
    gt = jnp.concatenate([gab_ref[0], gc_ref[0]], axis=-1).astype(F32)
    a = (y * (gt * (1.0 / (1.0 + jnp.exp(-gt))))).astype(BF16)
    o_ref[0] = h_ref[0] + g_ref[0] * jnp.dot(a, w_ref[...], preferred_element_type=F32)


def _outproj(ya, yb, yc, gate_ab, gate_c, h, g, w, tm):
    bx, tx, d = h.shape
    tok = lambda wd: pl.BlockSpec((1, tm, wd), lambda b, i: (b, i, 0))
    return pl.pallas_call(
        _outproj_kernel,
        grid=(bx, tx // tm),
        in_specs=[
            tok(MLA_WIDTH), tok(SWA_WIDTH), tok(DIFF_WIDTH), tok(GATE_AB), tok(DIFF_WIDTH), tok(d),
            pl.BlockSpec((1, 1, d), lambda b, i: (b, 0, 0)),
            pl.BlockSpec(w.shape, lambda b, i: (0, 0)),
        ],
        out_specs=tok(d),
        out_shape=jax.ShapeDtypeStruct((bx, tx, d), F32),
        compiler_params=_cparams(("parallel", "parallel")),
        name="gated_outproj_residual",
    )(ya, yb, yc, gate_ab, gate_c, h, g, w)


def _layer_weights(l, w_in, mla_w_uq, mla_w_ukv, mla_q_gain, mla_k_gain, w_out):
    d = w_in.shape[1]
    wi = w_in[l]
    o = np.cumsum([0, MLA_Q_RANK, MLA_KV_RANK, MLA_ROPE, MLA_WIDTH, SWA_WIDTH, SWA_KV_WIDTH,
                   SWA_KV_WIDTH, SWA_WIDTH, DIFF_WIDTH, DIFF_WIDTH, DIFF_WIDTH, DIFF_WIDTH]).tolist()
    seg = lambda j: wi[:, o[j]:o[j + 1]].astype(BF16)
    z = lambda n: jnp.zeros((d, n), BF16)
    w_p = jnp.concatenate([
        seg(0), seg(1), seg(2), z(LANE - MLA_ROPE), seg(5), seg(6), z(P_TILE - T0_SWA_V - SWA_KV_WIDTH),
        seg(4), seg(8), z(P_TILE - T1_DIF_Q - DIFF_WIDTH),
        seg(9), seg(10), seg(11),
        seg(3), seg(7)], axis=1)
    assert w_p.shape[1] == N_P_TILES * P_TILE
    pad = MLA_QK_PAD - MLA_QK
    w_uq = jnp.pad(mla_w_uq[l].reshape(MLA_Q_RANK, MLA_HEADS, MLA_QK), ((0, 0), (0, 0), (0, pad)))
    w_ukv = mla_w_ukv[l].reshape(MLA_KV_RANK, MLA_HEADS, MLA_NOPE + MLA_V)
    return {
        "w_p": w_p,
        "w_uq": w_uq.reshape(MLA_Q_RANK, MLA_HEADS * MLA_QK_PAD).astype(BF16),
        "w_kn": w_ukv[:, :, :MLA_NOPE].reshape(MLA_KV_RANK, MLA_HEADS * MLA_NOPE).astype(BF16),
        "w_v": w_ukv[:, :, MLA_NOPE:].reshape(MLA_KV_RANK, MLA_WIDTH).astype(BF16),
        "q_gain": jnp.pad(mla_q_gain[l], (0, pad)),
        "k_gain_nope": mla_k_gain[l][:MLA_NOPE],
        "k_gain_rope": jnp.pad(mla_k_gain[l][MLA_NOPE:], (0, LANE - MLA_ROPE)),
        "w_out": w_out[l].astype(BF16),
    }


def _rope_tables(n_tokens):
    t = np.arange(n_tokens)
    row, col = (t // GRID_W).astype(np.float64), (t % GRID_W).astype(np.float64)

    def cos_sin(rot_dim):
        n_freq = rot_dim // 4
        inv = np.power(ROPE_BASE, -np.arange(n_freq, dtype=np.float32) / n_freq).astype(np.float32)
        ang = np.concatenate([row[:, None].astype(np.float32) * inv, col[:, None].astype(np.float32) * inv], axis=-1)
        return np.cos(ang).astype(np.float32), np.sin(ang).astype(np.float32)

    c64, s64 = cos_sin(MLA_ROPE)
    c128, s128 = cos_sin(SWA_HEAD_DIM)
    zeros = np.zeros_like(s64)
    cd = np.concatenate([c64] * 4, axis=-1)
    sneg = np.concatenate([-s64, zeros, -s64, zeros], axis=-1)
    spos = np.concatenate([zeros, s64, zeros, s64], axis=-1)
    cs = np.concatenate([c128, c128], axis=-1)
    ss = np.concatenate([-s128, s128], axis=-1)
    return tuple(jnp.asarray(a, F32) for a in (cd, sneg, spos, cs, ss))


def _block_diag_ones(n, seg):
    idx = np.arange(n) // seg
    return jnp.asarray(idx[:, None] == idx[None, :], BF16)


def _pick(n, prefs):
    for t in prefs:
        if n % t == 0:
            return t
    return n


def kernel(x, c, ctx, c_ctx, norm_g, w_ada, b_ada, w_in, mla_q_norm, mla_w_uq, mla_kv_norm, mla_w_ukv,
           mla_q_gain, mla_k_gain, swa_q_gain, swa_k_gain, swa_sink, dif_q_gain, dif_k_gain,
           dif_lq1, dif_lk1, dif_lq2, dif_lk2, dif_out_gain, w_out):
    b, s, d = x.shape
    lc = ctx.shape[1]
    depth = w_in.shape[0]
    assert s % (2 * LANE) == 0 and s >= 4 * LANE and lc % LANE == 0 and d == MIX_WIDTH

    tables = _rope_tables(s)
    bd = _block_diag_ones(DIFF_WIDTH, DIFF_QK_DIM)

    n_rows = -(-(b + 1) // 8) * 8
    cvec = jnp.concatenate([c, c_ctx[None, :], jnp.zeros((n_rows - b - 1, d), F32)], axis=0)
    mod = _modulation(cvec, w_ada, b_ada)

    hx = x
    hc = ctx.reshape(1, b * lc, d)
    tm_x = _pick(s, (512, 256))
    tm_c = _pick(b * lc, (512, 256))
    tq_a = _pick(s, (1024, 512, 256))
    tq_b = _pick(s, (512, 256))
    tq_c = _pick(s, (512, 256))
    tk = 512

    for l in range(depth):
        need_ctx_out = l < depth - 1
        wl = _layer_weights(l, w_in, mla_w_uq, mla_w_ukv, mla_q_gain, mla_k_gain, w_out)
        wl.update(mla_q_norm=mla_q_norm[l], mla_kv_norm=mla_kv_norm[l], swa_q_gain=swa_q_gain[l],
                  swa_k_gain=swa_k_gain[l], dif_q_gain=jnp.tile(dif_q_gain[l], DIFF_WIDTH // DIFF_QK_DIM),
                  dif_k_gain=jnp.tile(dif_k_gain[l], DIFF_WIDTH // DIFF_QK_DIM), bd=bd)
        m = mod[l]
        sh_x, sc_x, g_x = (m[:b, j * d:(j + 1) * d].reshape(b, 1, d) for j in range(3))
        sh_c, sc_c, g_c = (m[b:b + 1, j * d:(j + 1) * d].reshape(1, 1, d) for j in range(3))
        ng = norm_g[l].reshape(1, d)
        lam_init = 0.8 - 0.6 * math.exp(-0.3 * l)
        lams = (dif_lq1[l], dif_lk1[l], dif_lq2[l], dif_lk2[l])
        sink = swa_sink[l]

        qa_x, ka_x, va_x, kb_x, vb_x, qb_x, qc_x, kc_x, vc_x, gc_x, gab_x = _proj_prep(
            hx, sc_x, sh_x, ng, tables, wl, tm_x)
        flat_c = _proj_prep(hc, sc_c, sh_c, ng, None, wl, tm_c)
        qa_c, ka_c, va_c, kb_c, vb_c, qb_c, qc_c, kc_c, vc_c = (t.reshape(b, lc, -1) for t in flat_c[:9])
        gc_c, gab_c = flat_c[9:]

        ya_x = _mla_attention(qa_x, [(ka_x, va_x), (ka_c, va_c)], tq_a, tk)
        yb_x = _swa_attention(qb_x, kb_x, vb_x, kb_c, vb_c, sink, tq_b, 256)
        yc_x = _diff_attention(qc_x, [(kc_x, vc_x), (kc_c, vc_c)], lams, dif_out_gain[l], lam_init, tq_c, tk)

        if need_ctx_out:
            ya_c = _mla_attention(qa_c, [(ka_c, va_c)], lc, tk)
            yb_c = _swa_ctx_attention(qb_c, kb_c, vb_c, sink)
            yc_c = _diff_attention(qc_c, [(kc_c, vc_c)], lams, dif_out_gain[l], lam_init, lc, tk)
            flat = lambda t: t.reshape(1, b * lc, -1)
            hc = _outproj(flat(ya_c), flat(yb_c), flat(yc_c), gab_c, gc_c, hc, g_c, wl["w_out"],
                          _pick(b * lc, (512, 256)))

        hx = _outproj(ya_x, yb_x, yc_x, gab_x, gc_x, hx, g_x, wl["w_out"], _pick(s, (512, 256)))

    return hx
```

```python
import functools
import math

import numpy as np
import jax
import jax.numpy as jnp
from jax import lax
from jax.experimental import pallas as pl
from jax.experimental.pallas import tpu as pltpu

F32 = jnp.float32
BF16 = jnp.bfloat16

GRID_W = 64
ROPE_BASE = 10000.0
EPS = 1e-6
LANE = 128

MLA_HEADS = 6
MLA_Q_RANK = 512
MLA_KV_RANK = 256
MLA_NOPE = 128
MLA_ROPE = 64
MLA_V = 128
MLA_QK = MLA_NOPE + MLA_ROPE
MLA_QK_PAD = 2 * LANE
MLA_WIDTH = MLA_HEADS * MLA_V

SWA_Q_HEADS = 6
SWA_KV_HEADS = 2
SWA_GROUP = SWA_Q_HEADS // SWA_KV_HEADS
SWA_HEAD_DIM = 128
SWA_WINDOW = 128
SWA_WIDTH = SWA_Q_HEADS * SWA_HEAD_DIM
SWA_KV_WIDTH = SWA_KV_HEADS * SWA_HEAD_DIM

DIFF_HEADS = 4
DIFF_QK_DIM = 64
DIFF_V_DIM = 2 * DIFF_QK_DIM
DIFF_WIDTH = DIFF_HEADS * DIFF_V_DIM

MIX_WIDTH = MLA_WIDTH + SWA_WIDTH + DIFF_WIDTH

LOG2E = math.log2(math.e)
NEG_BIG = -1e30

C_CQ = 0
C_CKV = C_CQ + MLA_Q_RANK
C_KR = C_CKV + MLA_KV_RANK
C_SWA_Q = 1024
C_SWA_K = C_SWA_Q + SWA_WIDTH
C_SWA_V = C_SWA_K + SWA_KV_WIDTH
C_DIF_Q = C_SWA_V + SWA_KV_WIDTH
C_DIF_K = C_DIF_Q + DIFF_WIDTH
C_DIF_V = C_DIF_K + DIFF_WIDTH
C_QKV_END = 4096
C_GATE = C_QKV_END
P_COLS = C_GATE + MIX_WIDTH

VMEM_LIMIT = 56 * 1024 * 1024


def _cparams(sem):
    return pltpu.CompilerParams(dimension_semantics=sem, vmem_limit_bytes=VMEM_LIMIT)


def _mod_kernel(c_ref, w_ref, b_ref, o_ref):
    c = c_ref[...]
    a = c * (1.0 / (1.0 + jnp.exp(-c)))
    o_ref[0] = jnp.dot(a, w_ref[0], preferred_element_type=F32, precision=lax.Precision.HIGHEST) + b_ref[0]


def _modulation(cvec, w_ada, b_ada):
    depth, d, n3 = w_ada.shape
    rows = cvec.shape[0]
    tn = 1536
    return pl.pallas_call(
        _mod_kernel,
        grid=(depth, n3 // tn),
        in_specs=[
            pl.BlockSpec((rows, d), lambda l, n: (0, 0)),
            pl.BlockSpec((1, d, tn), lambda l, n: (l, 0, n)),
            pl.BlockSpec((1, 1, tn), lambda l, n: (l, 0, n)),
        ],
        out_specs=pl.BlockSpec((1, rows, tn), lambda l, n: (l, 0, n)),
        out_shape=jax.ShapeDtypeStruct((depth, rows, n3), F32),
        compiler_params=_cparams(("parallel", "parallel")),
        name="adaln_modulation",
    )(cvec, w_ada, b_ada.reshape(depth, 1, n3))


def _inproj_kernel(x_ref, sc_ref, sh_ref, g_ref, w_ref, o_ref, xn_ref):
    j = pl.program_id(2)

    def project():
        o_ref[0] = jnp.dot(xn_ref[...], w_ref[...], preferred_element_type=F32).astype(BF16)

    @pl.when(j == 0)
    def _():
        x = x_ref[0]
        r = lax.rsqrt(jnp.mean(x * x, axis=-1, keepdims=True) + EPS)
        y = (x * r) * g_ref[...]
        xn_ref[...] = (y * (1.0 + sc_ref[0]) + sh_ref[0]).astype(BF16)
        project()

    @pl.when(j > 0)
    def _():
        project()


def _inproj(h, sc, sh, norm_g, w, tm, tn):
    bx, tx, d = h.shape
    n = w.shape[1]
    return pl.pallas_call(
        _inproj_kernel,
        grid=(bx, tx // tm, n // tn),
        in_specs=[
            pl.BlockSpec((1, tm, d), lambda b, i, j: (b, i, 0)),
            pl.BlockSpec((1, 1, d), lambda b, i, j: (b, 0, 0)),
            pl.BlockSpec((1, 1, d), lambda b, i, j: (b, 0, 0)),
            pl.BlockSpec((1, d), lambda b, i, j: (0, 0)),
            pl.BlockSpec((d, tn), lambda b, i, j: (0, j)),
        ],
        out_specs=pl.BlockSpec((1, tm, tn), lambda b, i, j: (b, i, j)),
        out_shape=jax.ShapeDtypeStruct((bx, tx, n), BF16),
        scratch_shapes=[pltpu.VMEM((tm, d), BF16)],
        compiler_params=_cparams(("parallel", "parallel", "arbitrary")),
        name="norm_mod_inproj",
    )(h, sc, sh, norm_g, w)


def _rope(x, cos, sin):
    return x * cos + pltpu.roll(x, LANE // 2, 1) * sin


def _prep_kernel(*refs, rope):
    if rope:
        (p_ref, cd_ref, sd_ref, cs_ref, ss_ref,
         qn_g_ref, wuq_ref, qgain_ref, kvn_g_ref, wkn_ref, wv_ref, kgn_ref, kgr_ref,
         sqg_ref, skg_ref, dqg_ref, dkg_ref, bd_ref,
         qa_ref, ka_ref, va_ref, qb_ref, kb_ref, qc_ref, kc_ref) = refs
        cd, sd, cs, ss = cd_ref[...], sd_ref[...], cs_ref[...], ss_ref[...]
        rope64 = lambda t: _rope(t, cd, sd)
        rope128 = lambda t: _rope(t, cs, ss)
    else:
        (p_ref, qn_g_ref, wuq_ref, qgain_ref, kvn_g_ref, wkn_ref, wv_ref, kgn_ref, kgr_ref,
         sqg_ref, skg_ref, dqg_ref, dkg_ref, bd_ref,
         qa_ref, ka_ref, va_ref, qb_ref, kb_ref, qc_ref, kc_ref) = refs
        rope64 = rope128 = lambda t: t

    def cols(lo, n):
        return p_ref[0, :, lo:lo + n].astype(F32)

    def rms(t, width):
        return lax.rsqrt(jnp.sum(t * t, axis=-1, keepdims=True) * (1.0 / width) + EPS)

    cq = cols(C_CQ, MLA_Q_RANK)
    cqn = ((cq * rms(cq, MLA_Q_RANK)) * qn_g_ref[...]).astype(BF16)
    q = jnp.dot(cqn, wuq_ref[...], preferred_element_type=F32)
    for h in range(MLA_HEADS):
        qh = q[:, h * MLA_QK_PAD:(h + 1) * MLA_QK_PAD]
        qh = (qh * rms(qh, MLA_QK)) * qgain_ref[...]
        qa_ref[0, :, h * MLA_QK_PAD:h * MLA_QK_PAD + LANE] = qh[:, :LANE].astype(BF16)
        qa_ref[0, :, h * MLA_QK_PAD + LANE:(h + 1) * MLA_QK_PAD] = rope64(qh[:, LANE:]).astype(BF16)

    ckv = cols(C_CKV, MLA_KV_RANK)
    ckvn = ((ckv * rms(ckv, MLA_KV_RANK)) * kvn_g_ref[...]).astype(BF16)
    kn = jnp.dot(ckvn, wkn_ref[...], preferred_element_type=F32)
    va_ref[0] = jnp.dot(ckvn, wv_ref[...], preferred_element_type=F32).astype(BF16)
    krp = cols(C_KR, LANE)
    kr_ssq = jnp.sum(krp * krp, axis=-1, keepdims=True)
    kr_rot = rope64(krp * kgr_ref[...])
    for h in range(MLA_HEADS):
        knh = kn[:, h * LANE:(h + 1) * LANE]
        r = lax.rsqrt((jnp.sum(knh * knh, axis=-1, keepdims=True) + kr_ssq) * (1.0 / MLA_QK) + EPS)
        ka_ref[0, :, h * MLA_QK_PAD:h * MLA_QK_PAD + LANE] = ((knh * r) * kgn_ref[...]).astype(BF16)
        ka_ref[0, :, h * MLA_QK_PAD + LANE:(h + 1) * MLA_QK_PAD] = (kr_rot * r).astype(BF16)

    for h in range(SWA_Q_HEADS):
        t = cols(C_SWA_Q + h * LANE, LANE)
        qb_ref[0, :, h * LANE:(h + 1) * LANE] = rope128((t * rms(t, SWA_HEAD_DIM)) * sqg_ref[...]).astype(BF16)
    for h in range(SWA_KV_HEADS):
        t = cols(C_SWA_K + h * LANE, LANE)
        kb_ref[0, :, h * LANE:(h + 1) * LANE] = rope128((t * rms(t, SWA_HEAD_DIM)) * skg_ref[...]).astype(BF16)

    def seg_norm(t, gain):
        t2 = t * t
        hi = t2.astype(BF16)
        lo = (t2 - hi.astype(F32)).astype(BF16)
        ssq = (jnp.dot(hi, bd_ref[...], preferred_element_type=F32)
               + jnp.dot(lo, bd_ref[...], preferred_element_type=F32))
        return (t * lax.rsqrt(ssq * (1.0 / DIFF_QK_DIM) + EPS)) * gain

    tq = seg_norm(cols(C_DIF_Q, DIFF_WIDTH), dqg_ref[...])
    tk = seg_norm(cols(C_DIF_K, DIFF_WIDTH), dkg_ref[...])
    for j in range(DIFF_WIDTH // LANE):
        sl = slice(j * LANE, (j + 1) * LANE)
        qc_ref[0, :, sl] = rope64(tq[:, sl]).astype(BF16)
        kc_ref[0, :, sl] = rope64(tk[:, sl]).astype(BF16)


def _prep(p, tables, wl, tm):
    bx, tx, _ = p.shape
    rope = tables is not None
    row = lambda a: a.reshape(1, -1)
    full = lambda a: pl.BlockSpec(a.shape, lambda b, i: (0,) * a.ndim)
    consts = [row(wl["mla_q_norm"]), wl["w_uq"], row(wl["q_gain"]), row(wl["mla_kv_norm"]), wl["w_kn"], wl["w_v"],
              row(wl["k_gain_nope"]), row(wl["k_gain_rope"]), row(wl["swa_q_gain"]), row(wl["swa_k_gain"]),
              row(wl["dif_q_gain"]), row(wl["dif_k_gain"]), wl["bd"]]
    ins = [p]
    in_specs = [pl.BlockSpec((1, tm, C_QKV_END), lambda b, i: (b, i, 0))]
    if rope:
        ins += list(tables)
        in_specs += [pl.BlockSpec((tm, LANE), lambda b, i: (i, 0))] * len(tables)
    ins += consts
    in_specs += [full(a) for a in consts]
    widths = [MLA_HEADS * MLA_QK_PAD, MLA_HEADS * MLA_QK_PAD, MLA_WIDTH, SWA_WIDTH,
              SWA_KV_WIDTH, DIFF_WIDTH, DIFF_WIDTH]
    return pl.pallas_call(
        functools.partial(_prep_kernel, rope=rope),
        grid=(bx, tx // tm),
        in_specs=in_specs,
        out_specs=[pl.BlockSpec((1, tm, w), lambda b, i: (b, i, 0)) for w in widths],
        out_shape=[jax.ShapeDtypeStruct((bx, tx, w), BF16) for w in widths],
        compiler_params=_cparams(("parallel", "parallel")),
        name="head_prep_rope" if rope else "head_prep",
    )(*ins)


def _fold_lanes(t, op):
    parts = [t[:, c * LANE:(c + 1) * LANE] for c in range(t.shape[1] // LANE)]
    return functools.reduce(op, parts)


def _key_chunks(srcs, tk):
    out = []
    for k_ref, v_ref in srcs:
        n = k_ref.shape[0]
        for c0 in range(0, n, tk):
            out.append((k_ref, v_ref, c0, min(tk, n - c0)))
    return out


def _dense_attention(q, srcs, s_scr, tk, sink=None):
    chunks = _key_chunks(srcs, tk)
    mpart = None
    off = 0
    for k_ref, _, c0, n in chunks:
        s = lax.dot_general(q, k_ref[c0:c0 + n, :], (((1,), (1,)), ((), ())), preferred_element_type=F32)
        s_scr[:, off:off + n] = s
        f = _fold_lanes(s, jnp.maximum)
        mpart = f if mpart is None else jnp.maximum(mpart, f)
        off += n
    m = jnp.max(mpart, axis=-1, keepdims=True)
    if sink is not None:
        m = jnp.maximum(m, sink)
    lpart = None
    acc = None
    off = 0
    for _, v_ref, c0, n in chunks:
        p = jnp.exp2(s_scr[:, off:off + n] - m)
        f = _fold_lanes(p, jnp.add)
        lpart = f if lpart is None else lpart + f
        pv = jnp.dot(p.astype(BF16), v_ref[c0:c0 + n, :], preferred_element_type=F32)
        acc = pv if acc is None else acc + pv
        off += n
    l = jnp.sum(lpart, axis=-1, keepdims=True)
    if sink is not None:
        l = l + jnp.exp2(sink - m)
    return acc * (1.0 / l)


class _View:
    def __init__(self, ref, lead):
        self.ref, self.lead = ref, lead
        self.shape = ref.shape[len(lead):]

    def __getitem__(self, idx):
        return self.ref[self.lead + idx]


def _score_pass(q, k_views, s_ref, m_ref, tk):
    mpart = None
    off = 0
    for k_ref in k_views:
        n_keys = k_ref.shape[0]
        for c0 in range(0, n_keys, tk):
            n = min(tk, n_keys - c0)
            s = lax.dot_general(q, k_ref[c0:c0 + n, :], (((1,), (1,)), ((), ())), preferred_element_type=F32)
            s_ref[:, off:off + n] = s
            f = _fold_lanes(s, jnp.maximum)
            mpart = f if mpart is None else jnp.maximum(mpart, f)
            off += n
            yield
    m_ref[...] = jnp.broadcast_to(jnp.max(mpart, axis=-1, keepdims=True), m_ref.shape)
    yield


def _prob_pass(s_ref, m_ref, v_views, tk, result):
    mb = m_ref[...]
    lpart = None
    acc = None
    off = 0
    for v_ref in v_views:
        n_keys = v_ref.shape[0]
        for c0 in range(0, n_keys, tk):
            n = min(tk, n_keys - c0)
            ps = []
            for c in range(n // LANE):
                pc = jnp.exp2(s_ref[:, off + c * LANE:off + (c + 1) * LANE] - mb)
                lpart = pc if lpart is None else lpart + pc
                ps.append(pc.astype(BF16))
            pv = jnp.dot(jnp.concatenate(ps, axis=1), v_ref[c0:c0 + n, :], preferred_element_type=F32)
            acc = pv if acc is None else acc + pv
            off += n
            yield
    result.append(acc * (1.0 / jnp.sum(lpart, axis=-1, keepdims=True)))
    yield


def _interleave(*gens):
    live = list(gens)
    while live:
        for g in list(live):
            try:
                next(g)
            except StopIteration:
                live.remove(g)


def _pipelined_attention_kernel(*refs, n_src, n_extra, tk, n_units, load_q, finish):
    q_ref = refs[0]
    k_views = [_View(r, (0,)) for r in refs[1:1 + n_src]]
    v_views = [_View(r, (0,)) for r in refs[1 + n_src:1 + 2 * n_src]]
    extra = refs[1 + 2 * n_src:1 + 2 * n_src + n_extra]
    o_ref, s_a, s_b, m_a, m_b = refs[1 + 2 * n_src + n_extra:]
    t = pl.program_id(0)
    bufs = ((s_a, m_a), (s_b, m_b))

    def score(slot):
        return _score_pass(load_q(q_ref), k_views, bufs[slot][0], bufs[slot][1], tk)

    def prob(slot):
        result = []
        yield from _prob_pass(bufs[slot][0], bufs[slot][1], v_views, tk, result)
        finish(result[0], o_ref, extra)

    @pl.when(t == 0)
    def _():
        _interleave(score(0))

    for parity in range(2):
        @pl.when((t > 0) & (t < n_units) & (t % 2 == parity))
        def _():
            _interleave(score(parity), prob(1 - parity))

    @pl.when(t == n_units)
    def _():
        _interleave(prob((n_units - 1) % 2))


def _pipelined_attention(q, q_width, k_list, v_list, extra, heads, out_width, tq, tk, rows_per_q, load_q, finish, name):
    b, t_q, _ = q.shape
    nq = t_q // tq
    n_units = b * heads * nq

    def unit(t):
        u = jnp.minimum(t, n_units - 1)
        return u // (heads * nq), (u // nq) % heads, u % nq

    def q_map(t):
        bi, h, i = unit(t)
        return bi, i, h

    def o_map(t):
        bi, h, i = unit(jnp.maximum(t - 1, 0))
        return bi, i, h

    def kv_map(col0, lag):
        def f(t):
            bi, h, _ = unit(jnp.maximum(t - lag, 0))
            return bi, 0, col0 + h
        return f

    ins = [q]
    in_specs = [pl.BlockSpec((1, tq, q_width), q_map)]
    total = 0
    for arr, width, col0 in k_list:
        total += arr.shape[1]
        ins.append(arr)
        in_specs.append(pl.BlockSpec((1, arr.shape[1], width), kv_map(col0, 0)))
    for arr, width, col0 in v_list:
        ins.append(arr)
        in_specs.append(pl.BlockSpec((1, arr.shape[1], width), kv_map(col0, 1)))
    ins += list(extra)
    in_specs += [pl.BlockSpec(a.shape, lambda t: (0, 0)) for a in extra]
    m_rows = rows_per_q * tq
    return pl.pallas_call(
        functools.partial(_pipelined_attention_kernel, n_src=len(k_list), n_extra=len(extra), tk=tk,
                          n_units=n_units, load_q=load_q, finish=finish),
        grid=(n_units + 1,),
        in_specs=in_specs,
        out_specs=pl.BlockSpec((1, tq, out_width), o_map),
        out_shape=jax.ShapeDtypeStruct((b, t_q, heads * out_width), BF16),
        scratch_shapes=[pltpu.VMEM((m_rows, total), F32), pltpu.VMEM((m_rows, total), F32),
                        pltpu.VMEM((m_rows, LANE), F32), pltpu.VMEM((m_rows, LANE), F32)],
        compiler_params=_cparams(("arbitrary",)),
        name=name,
    )(*ins)


def _mla_load_q(q_ref):
    return q_ref[0]


def _mla_finish(o, o_ref, extra):
    o_ref[0] = o.astype(BF16)


def _mla_attention(q, kv_list, tq, tk):
    return _pipelined_attention(
        q, MLA_QK_PAD, [(k, MLA_QK_PAD, 0) for k, _ in kv_list], [(v, MLA_V, 0) for _, v in kv_list], (),
        MLA_HEADS, MLA_V, tq, tk, 1, _mla_load_q, _mla_finish, "mla_attention")


def _diff_load_q(q_ref):
    q = q_ref[0]
    first = (lax.broadcasted_iota(jnp.int32, q.shape, 1) & (DIFF_QK_DIM // 2)) == 0
    zero = jnp.zeros_like(q)
    return jnp.concatenate([jnp.where(first, q, zero), jnp.where(first, zero, q)], axis=0)


def _diff_finish(o, o_ref, extra, lam_init):
    lq1_ref, lk1_ref, lq2_ref, lk2_ref, og_ref = extra
    tq = o.shape[0] // 2
    lam = (jnp.exp(jnp.sum(lq1_ref[...] * lk1_ref[...], axis=-1, keepdims=True))
           - jnp.exp(jnp.sum(lq2_ref[...] * lk2_ref[...], axis=-1, keepdims=True)) + lam_init)
    d = o[:tq] - lam * o[tq:]
    r = lax.rsqrt(jnp.mean(d * d, axis=-1, keepdims=True) + EPS)
    o_ref[0] = (((d * r) * og_ref[...]) * (1.0 - lam_init)).astype(BF16)


def _diff_attention(q, kv_list, v_col0, lams, out_gain, lam_init, tq, tk):
    extra = [a.reshape(1, -1) for a in lams] + [out_gain.reshape(1, -1)]
    return _pipelined_attention(
        q, LANE, [(k, LANE, 0) for k, _ in kv_list], [(pv, LANE, v_col0) for _, pv in kv_list], extra,
        DIFF_HEADS, DIFF_V_DIM, tq, tk, 2, _diff_load_q, functools.partial(_diff_finish, lam_init=lam_init),
        "diff_attention")


def _stack_heads(q_ref):
    return jnp.concatenate([q_ref[0, :, g * LANE:(g + 1) * LANE] for g in range(SWA_GROUP)], axis=0)


def _sink_rows(sink_ref, h, tq):
    return jnp.concatenate(
        [jnp.full((tq, 1), sink_ref[h * SWA_GROUP + g] * LOG2E, F32) for g in range(SWA_GROUP)], axis=0)


def _swa_ctx_kernel(sink_ref, q_ref, k_ref, v_ref, o_ref, s_scr):
    tq = q_ref.shape[1]
    o = _dense_attention(_stack_heads(q_ref), [(_View(k_ref, (0,)), _View(v_ref, (0,)))], s_scr, k_ref.shape[1],
                         sink=_sink_rows(sink_ref, pl.program_id(1), tq))
    for g in range(SWA_GROUP):
        o_ref[0, :, g * LANE:(g + 1) * LANE] = o[g * tq:(g + 1) * tq].astype(BF16)


def _swa_window_kernel(sink_ref, q_ref, k_ref, v_ref, kc_ref, vc_ref, o_ref, *, sub):
    i = pl.program_id(1)
    tq = q_ref.shape[1]
    s_len = k_ref.shape[1]
    win = sub + 2 * SWA_WINDOW
    dn = (((1,), (1,)), ((), ()))
    for j in range(tq // sub):
        q0 = i * tq + j * sub
        start = pl.multiple_of(jnp.clip(q0 - SWA_WINDOW, 0, s_len - win), LANE)
        rel = (lax.broadcasted_iota(jnp.int32, (sub, win), 1) - lax.broadcasted_iota(jnp.int32, (sub, win), 0)
               + (start - q0))
        bias1 = jnp.where(jnp.abs(rel) <= SWA_WINDOW, 0.0, NEG_BIG).astype(F32)
        bias = jnp.concatenate([bias1] * SWA_GROUP, axis=0)
        for h in range(SWA_KV_HEADS):
            qq = jnp.concatenate(
                [q_ref[0, j * sub:(j + 1) * sub, (h * SWA_GROUP + g) * LANE:(h * SWA_GROUP + g + 1) * LANE]
                 for g in range(SWA_GROUP)], axis=0)
            hl = slice(h * LANE, (h + 1) * LANE)
            s_w = lax.dot_general(qq, k_ref[0, pl.ds(start, win), hl], dn, preferred_element_type=F32) + bias
            s_c = lax.dot_general(qq, kc_ref[0, :, hl], dn, preferred_element_type=F32)
            sink = _sink_rows(sink_ref, h, sub)
            mf = jnp.maximum(_fold_lanes(s_w, jnp.maximum), _fold_lanes(s_c, jnp.maximum))
            m = jnp.maximum(jnp.max(mf, axis=-1, keepdims=True), sink)
            p_w = jnp.exp2(s_w - m)
            p_c = jnp.exp2(s_c - m)
            lf = _fold_lanes(p_w, jnp.add) + _fold_lanes(p_c, jnp.add)
            l = jnp.sum(lf, axis=-1, keepdims=True) + jnp.exp2(sink - m)
            o = (jnp.dot(p_w.astype(BF16), v_ref[0, pl.ds(start, win), hl], preferred_element_type=F32)
                 + jnp.dot(p_c.astype(BF16), vc_ref[0, :, hl], preferred_element_type=F32)) * (1.0 / l)
            for g in range(SWA_GROUP):
                col = (h * SWA_GROUP + g) * LANE
                o_ref[0, j * sub:(j + 1) * sub, col:col + LANE] = o[g * sub:(g + 1) * sub].astype(BF16)


def _swa_attention(q, k, pv, kc, pvc, sink, tq, sub):
    b, s_len, _ = q.shape
    lc = kc.shape[1]
    vblk = C_SWA_V // SWA_KV_WIDTH
    assert C_SWA_V % SWA_KV_WIDTH == 0
    qspec = pl.BlockSpec((1, tq, SWA_WIDTH), lambda bi, i: (bi, i, 0))
    return pl.pallas_call(
        functools.partial(_swa_window_kernel, sub=sub),
        grid=(b, s_len // tq),
        in_specs=[
            pl.BlockSpec(memory_space=pltpu.SMEM),
            qspec,
            pl.BlockSpec((1, s_len, SWA_KV_WIDTH), lambda bi, i: (bi, 0, 0)),
            pl.BlockSpec((1, s_len, SWA_KV_WIDTH), lambda bi, i: (bi, 0, vblk)),
            pl.BlockSpec((1, lc, SWA_KV_WIDTH), lambda bi, i: (bi, 0, 0)),
            pl.BlockSpec((1, lc, SWA_KV_WIDTH), lambda bi, i: (bi, 0, vblk)),
        ],
        out_specs=qspec,
        out_shape=jax.ShapeDtypeStruct((b, s_len, SWA_WIDTH), BF16),
        compiler_params=_cparams(("parallel", "arbitrary")),
        name="swa_window_attention",
    )(sink, q, k, pv, kc, pvc)


def _swa_ctx_attention(q, k, pv, sink):
    b, lc, _ = q.shape
    vblk = C_SWA_V // LANE
    qspec = pl.BlockSpec((1, lc, SWA_GROUP * LANE), lambda bi, h: (bi, 0, h))
    return pl.pallas_call(
        _swa_ctx_kernel,
        grid=(b, SWA_KV_HEADS),
        in_specs=[
            pl.BlockSpec(memory_space=pltpu.SMEM),
            qspec,
            pl.BlockSpec((1, lc, LANE), lambda bi, h: (bi, 0, h)),
            pl.BlockSpec((1, lc, LANE), lambda bi, h: (bi, 0, vblk + h)),
        ],
        out_specs=qspec,
        out_shape=jax.ShapeDtypeStruct((b, lc, SWA_WIDTH), BF16),
        scratch_shapes=[pltpu.VMEM((SWA_GROUP * lc, lc), F32)],
        compiler_params=_cparams(("parallel", "parallel")),
        name="swa_ctx_attention",
    )(sink, q, k, pv)


def _outproj_kernel(ya_ref, yb_ref, yc_ref, gate_ref, h_ref, g_ref, w_ref, o_ref):
    y = jnp.concatenate([ya_ref[0], yb_ref[0], yc_ref[0]], axis=-1).astype(F32)
    gt = gate_ref[0].astype(F32)
    a = (y * (gt * (1.0 / (1.0 + jnp.exp(-gt))))).astype(BF16)
    o_ref[0] = h_ref[0] + g_ref[0] * jnp.dot(a, w_ref[...], preferred_element_type=F32)


def _outproj(ya, yb, yc, p, h, g, w, tm):
    bx, tx, d = h.shape
    tok = lambda wd: pl.BlockSpec((1, tm, wd), lambda b, i: (b, i, 0))
    return pl.pallas_call(
        _outproj_kernel,
        grid=(bx, tx // tm),
        in_specs=[
            tok(MLA_WIDTH), tok(SWA_WIDTH), tok(DIFF_WIDTH),
            pl.BlockSpec((1, tm, MIX_WIDTH), lambda b, i: (b, i, C_GATE // MIX_WIDTH)),
            tok(d),
            pl.BlockSpec((1, 1, d), lambda b, i: (b, 0, 0)),
            pl.BlockSpec(w.shape, lambda b, i: (0, 0)),
        ],
        out_specs=tok(d),
        out_shape=jax.ShapeDtypeStruct((bx, tx, d), F32),
        compiler_params=_cparams(("parallel", "parallel")),
        name="gated_outproj_residual",
    )(ya, yb, yc, p, h, g, w)


def _swap_mid32(a):
    g = a.reshape(a.shape[:-1] + (a.shape[-1] // LANE, 4, LANE // 4))
    return jnp.stack([g[..., 0, :], g[..., 2, :], g[..., 1, :], g[..., 3, :]], axis=-2).reshape(a.shape)


def _layer_weights(l, w_in, mla_w_uq, mla_w_ukv, mla_q_gain, mla_k_gain, w_out):
    d = w_in.shape[1]
    wi = w_in[l]
    o = np.cumsum([0, MLA_Q_RANK, MLA_KV_RANK, MLA_ROPE, MLA_WIDTH, SWA_WIDTH, SWA_KV_WIDTH,
                   SWA_KV_WIDTH, SWA_WIDTH, DIFF_WIDTH, DIFF_WIDTH, DIFF_WIDTH, DIFF_WIDTH]).tolist()
    seg = lambda j: wi[:, o[j]:o[j + 1]]
    z = lambda n: jnp.zeros((d, n), wi.dtype)
    kr_blk = _swap_mid32(jnp.concatenate([seg(2), z(LANE - MLA_ROPE)], axis=1))
    w_p = jnp.concatenate([seg(0), seg(1), kr_blk, z(C_SWA_Q - C_KR - LANE),
                           seg(4), seg(5), seg(6), _swap_mid32(seg(8)), _swap_mid32(seg(9)), seg(10),
                           z(C_QKV_END - C_DIF_V - DIFF_WIDTH),
                           seg(3), seg(7), seg(11)], axis=1).astype(BF16)
    pad = MLA_QK_PAD - MLA_QK
    w_uq = jnp.pad(mla_w_uq[l].reshape(MLA_Q_RANK, MLA_HEADS, MLA_QK), ((0, 0), (0, 0), (0, pad)))
    w_uq = jnp.concatenate([w_uq[..., :LANE], _swap_mid32(w_uq[..., LANE:])], axis=-1)
    q_gain = jnp.pad(mla_q_gain[l], (0, pad))
    w_ukv = mla_w_ukv[l].reshape(MLA_KV_RANK, MLA_HEADS, MLA_NOPE + MLA_V)
    return {
        "w_p": w_p,
        "w_uq": w_uq.reshape(MLA_Q_RANK, MLA_HEADS * MLA_QK_PAD).astype(BF16),
        "w_kn": w_ukv[:, :, :MLA_NOPE].reshape(MLA_KV_RANK, MLA_HEADS * MLA_NOPE).astype(BF16),
        "w_v": w_ukv[:, :, MLA_NOPE:].reshape(MLA_KV_RANK, MLA_WIDTH).astype(BF16),
        "q_gain": jnp.concatenate([q_gain[:LANE], _swap_mid32(q_gain[LANE:])]),
        "k_gain_nope": mla_k_gain[l][:MLA_NOPE],
        "k_gain_rope": _swap_mid32(jnp.pad(mla_k_gain[l][MLA_NOPE:], (0, LANE - MLA_ROPE))),
        "w_out": w_out[l].astype(BF16),
    }


def _rope_tables(n_tokens):
    t = np.arange(n_tokens)
    row, col = (t // GRID_W).astype(np.float64), (t % GRID_W).astype(np.float64)

    def cos_sin(rot_dim):
        n_freq = rot_dim // 4
        inv = np.power(ROPE_BASE, -np.arange(n_freq, dtype=np.float32) / n_freq).astype(np.float32)
        ang = np.concatenate([row[:, None].astype(np.float32) * inv, col[:, None].astype(np.float32) * inv], axis=-1)
        return np.cos(ang).astype(np.float32), np.sin(ang).astype(np.float32)

    c64, s64 = cos_sin(MLA_ROPE)
    c128, s128 = cos_sin(SWA_HEAD_DIM)
    cd = np.concatenate([c64] * 4, axis=-1)
    sd = np.concatenate([-s64, -s64, s64, s64], axis=-1)
    cs = np.concatenate([c128, c128], axis=-1)
    ss = np.concatenate([-s128, s128], axis=-1)
    return tuple(jnp.asarray(a, F32) for a in (cd, sd, cs, ss))


def _segment_ones(n):
    lane = np.arange(n)
    idx = (lane // LANE) * 2 + (lane // (LANE // 4)) % 2
    return jnp.asarray(idx[:, None] == idx[None, :], BF16)


def _pick(n, prefs):
    for t in prefs:
        if n % t == 0:
            return t
    return n


def kernel(x, c, ctx, c_ctx, norm_g, w_ada, b_ada, w_in, mla_q_norm, mla_w_uq, mla_kv_norm, mla_w_ukv,
           mla_q_gain, mla_k_gain, swa_q_gain, swa_k_gain, swa_sink, dif_q_gain, dif_k_gain,
           dif_lq1, dif_lk1, dif_lq2, dif_lk2, dif_out_gain, w_out):
    b, s, d = x.shape
    lc = ctx.shape[1]
    depth = w_in.shape[0]
    assert s % (2 * LANE) == 0 and s >= 4 * LANE and lc % LANE == 0 and d == MIX_WIDTH

    tables = _rope_tables(s)
    bd = _segment_ones(DIFF_WIDTH)

    n_rows = -(-(b + 1) // 8) * 8
    cvec = jnp.concatenate([c, c_ctx[None, :], jnp.zeros((n_rows - b - 1, d), F32)], axis=0)
    mod = _modulation(cvec, w_ada, b_ada)

    hx = x
    hc = ctx.reshape(1, b * lc, d)
    tm_x = _pick(s, (1024, 512, 256))
    tm_c = _pick(b * lc, (1024, 512, 256))
    tq_a = _pick(s, (1024, 512, 256))
    tq_b = _pick(s, (512, 256))
    tq_c = _pick(s, (512, 256))
    tk = 512
    tn = 1536

    for l in range(depth):
        need_ctx_out = l < depth - 1
        wl = _layer_weights(l, w_in, mla_w_uq, mla_w_ukv, mla_q_gain, mla_k_gain, w_out)
        wl.update(mla_q_norm=mla_q_norm[l], mla_kv_norm=mla_kv_norm[l],
                  q_gain=wl["q_gain"] * (MLA_QK ** -0.5 * LOG2E),
                  swa_q_gain=swa_q_gain[l] * (SWA_HEAD_DIM ** -0.5 * LOG2E), swa_k_gain=swa_k_gain[l],
                  dif_q_gain=_swap_mid32(jnp.tile(dif_q_gain[l], DIFF_WIDTH // DIFF_QK_DIM))
                  * (DIFF_QK_DIM ** -0.5 * LOG2E),
                  dif_k_gain=_swap_mid32(jnp.tile(dif_k_gain[l], DIFF_WIDTH // DIFF_QK_DIM)), bd=bd)
        m = mod[l]
        sh_x, sc_x, g_x = (m[:b, j * d:(j + 1) * d].reshape(b, 1, d) for j in range(3))
        sh_c, sc_c, g_c = (m[b:b + 1, j * d:(j + 1) * d].reshape(1, 1, d) for j in range(3))
        ng = norm_g[l].reshape(1, d)
        lam_init = 0.8 - 0.6 * math.exp(-0.3 * l)
        lams = (dif_lq1[l], dif_lk1[l], dif_lq2[l], dif_lk2[l])
        sink = swa_sink[l]

        px = _inproj(hx, sc_x, sh_x, ng, wl["w_p"], tm_x, tn)
        pc = _inproj(hc, sc_c, sh_c, ng, wl["w_p"], tm_c, tn)
        qa_x, ka_x, va_x, qb_x, kb_x, qc_x, kc_x = _prep(px, tables, wl, _pick(s, (512, 256)))
        qa_c, ka_c, va_c, qb_c, kb_c, qc_c, kc_c = (
            t.reshape(b, lc, -1) for t in _prep(pc, None, wl, _pick(b * lc, (512, 256))))
        pcb = pc.reshape(b, lc, P_COLS)
        vblk_c = C_DIF_V // LANE

        ya_x = _mla_attention(qa_x, [(ka_x, va_x), (ka_c, va_c)], tq_a, tk)
        yb_x = _swa_attention(qb_x, kb_x, px, kb_c, pcb, sink, tq_b, 256)
        yc_x = _diff_attention(qc_x, [(kc_x, px), (kc_c, pcb)], vblk_c, lams, dif_out_gain[l], lam_init, tq_c, tk)

        if need_ctx_out:
            ya_c = _mla_attention(qa_c, [(ka_c, va_c)], lc, tk)
            yb_c = _swa_ctx_attention(qb_c, kb_c, pcb, sink)
            yc_c = _diff_attention(qc_c, [(kc_c, pcb)], vblk_c, lams, dif_out_gain[l], lam_init, lc, tk)
            flat = lambda t: t.reshape(1, b * lc, -1)
            hc = _outproj(flat(ya_c), flat(yb_c), flat(yc_c), pc, hc, g_c, wl["w_out"], _pick(b * lc, (512, 256)))

        hx = _outproj(ya_x, yb_x, yc_x, px, hx, g_x, wl["w_out"], _pick(s, (512, 256)))

    return hx
```

```python
import functools
import math

import numpy as np
import jax
import jax.numpy as jnp
from jax import lax
from jax.experimental import pallas as pl
from jax.experimental.pallas import tpu as pltpu

F32 = jnp.float32
BF16 = jnp.bfloat16

GRID_W = 64
ROPE_BASE = 10000.0
EPS = 1e-6
LANE = 128

MLA_HEADS = 6
MLA_Q_RANK = 512
MLA_KV_RANK = 256
MLA_NOPE = 128
MLA_ROPE = 64
MLA_V = 128
MLA_QK = MLA_NOPE + MLA_ROPE
MLA_QK_PAD = 2 * LANE
MLA_WIDTH = MLA_HEADS * MLA_V

SWA_Q_HEADS = 6
SWA_KV_HEADS = 2
SWA_GROUP = SWA_Q_HEADS // SWA_KV_HEADS
SWA_HEAD_DIM = 128
SWA_WINDOW = 128
SWA_WIDTH = SWA_Q_HEADS * SWA_HEAD_DIM
SWA_KV_WIDTH = SWA_KV_HEADS * SWA_HEAD_DIM

DIFF_HEADS = 4
DIFF_QK_DIM = 64
DIFF_V_DIM = 2 * DIFF_QK_DIM
DIFF_WIDTH = DIFF_HEADS * DIFF_V_DIM

MIX_WIDTH = MLA_WIDTH + SWA_WIDTH + DIFF_WIDTH

LOG2E = math.log2(math.e)
NEG_BIG = -1e30

C_CQ = 0
C_CKV = C_CQ + MLA_Q_RANK
C_KR = C_CKV + MLA_KV_RANK
C_SWA_Q = 1024
C_SWA_K = C_SWA_Q + SWA_WIDTH
C_SWA_V = C_SWA_K + SWA_KV_WIDTH
C_DIF_Q = C_SWA_V + SWA_KV_WIDTH
C_DIF_K = C_DIF_Q + DIFF_WIDTH
C_DIF_V = C_DIF_K + DIFF_WIDTH
C_QKV_END = 4096
C_GATE = C_QKV_END
P_COLS = C_GATE + MIX_WIDTH

VMEM_LIMIT = 56 * 1024 * 1024


def _cparams(sem):
    return pltpu.CompilerParams(dimension_semantics=sem, vmem_limit_bytes=VMEM_LIMIT)


def _mod_kernel(c_ref, w_ref, b_ref, o_ref):
    c = c_ref[...]
    a = c * (1.0 / (1.0 + jnp.exp(-c)))
    o_ref[0] = jnp.dot(a, w_ref[0], preferred_element_type=F32, precision=lax.Precision.HIGHEST) + b_ref[0]


def _modulation(cvec, w_ada, b_ada):
    depth, d, n3 = w_ada.shape
    rows = cvec.shape[0]
    tn = 1536
    return pl.pallas_call(
        _mod_kernel,
        grid=(depth, n3 // tn),
        in_specs=[
            pl.BlockSpec((rows, d), lambda l, n: (0, 0)),
            pl.BlockSpec((1, d, tn), lambda l, n: (l, 0, n)),
            pl.BlockSpec((1, 1, tn), lambda l, n: (l, 0, n)),
        ],
        out_specs=pl.BlockSpec((1, rows, tn), lambda l, n: (l, 0, n)),
        out_shape=jax.ShapeDtypeStruct((depth, rows, n3), F32),
        compiler_params=_cparams(("parallel", "parallel")),
        name="adaln_modulation",
    )(cvec, w_ada, b_ada.reshape(depth, 1, n3))


def _inproj_kernel(x_ref, sc_ref, sh_ref, g_ref, w_ref, o_ref, xn_ref):
    j = pl.program_id(2)

    def project():
        o_ref[0] = jnp.dot(xn_ref[...], w_ref[...], preferred_element_type=F32).astype(BF16)

    @pl.when(j == 0)
    def _():
        x = x_ref[0]
        r = lax.rsqrt(jnp.mean(x * x, axis=-1, keepdims=True) + EPS)
        y = (x * r) * g_ref[...]
        xn_ref[...] = (y * (1.0 + sc_ref[0]) + sh_ref[0]).astype(BF16)
        project()

    @pl.when(j > 0)
    def _():
        project()


def _inproj(h, sc, sh, norm_g, w, tm, tn):
    bx, tx, d = h.shape
    n = w.shape[1]
    return pl.pallas_call(
        _inproj_kernel,
        grid=(bx, tx // tm, n // tn),
        in_specs=[
            pl.BlockSpec((1, tm, d), lambda b, i, j: (b, i, 0)),
            pl.BlockSpec((1, 1, d), lambda b, i, j: (b, 0, 0)),
            pl.BlockSpec((1, 1, d), lambda b, i, j: (b, 0, 0)),
            pl.BlockSpec((1, d), lambda b, i, j: (0, 0)),
            pl.BlockSpec((d, tn), lambda b, i, j: (0, j)),
        ],
        out_specs=pl.BlockSpec((1, tm, tn), lambda b, i, j: (b, i, j)),
        out_shape=jax.ShapeDtypeStruct((bx, tx, n), BF16),
        scratch_shapes=[pltpu.VMEM((tm, d), BF16)],
        compiler_params=_cparams(("parallel", "parallel", "arbitrary")),
        name="norm_mod_inproj",
    )(h, sc, sh, norm_g, w)


def _rope(x, cos, sin):
    return x * cos + pltpu.roll(x, LANE // 2, 1) * sin


def _prep_kernel(*refs, rope):
    if rope:
        (p_ref, cd_ref, sd_ref, cs_ref, ss_ref,
         qn_g_ref, wuq_ref, qgain_ref, kvn_g_ref, wkn_ref, wv_ref, kgn_ref, kgr_ref,
         sqg_ref, skg_ref, dqg_ref, dkg_ref, bd_ref,
         qa_ref, ka_ref, va_ref, qb_ref, kb_ref, qc_ref, kc_ref) = refs
        cd, sd, cs, ss = cd_ref[...], sd_ref[...], cs_ref[...], ss_ref[...]
        rope64 = lambda t: _rope(t, cd, sd)
        rope128 = lambda t: _rope(t, cs, ss)
    else:
        (p_ref, qn_g_ref, wuq_ref, qgain_ref, kvn_g_ref, wkn_ref, wv_ref, kgn_ref, kgr_ref,
         sqg_ref, skg_ref, dqg_ref, dkg_ref, bd_ref,
         qa_ref, ka_ref, va_ref, qb_ref, kb_ref, qc_ref, kc_ref) = refs
        rope64 = rope128 = lambda t: t

    def cols(lo, n):
        return p_ref[0, :, lo:lo + n].astype(F32)

    def rms(t, width):
        return lax.rsqrt(jnp.sum(t * t, axis=-1, keepdims=True) * (1.0 / width) + EPS)

    cq = cols(C_CQ, MLA_Q_RANK)
    cqn = ((cq * rms(cq, MLA_Q_RANK)) * qn_g_ref[...]).astype(BF16)
    q = jnp.dot(cqn, wuq_ref[...], preferred_element_type=F32)
    for h in range(MLA_HEADS):
        qh = q[:, h * MLA_QK_PAD:(h + 1) * MLA_QK_PAD]
        qh = (qh * rms(qh, MLA_QK)) * qgain_ref[...]
        qa_ref[0, :, h * MLA_QK_PAD:h * MLA_QK_PAD + LANE] = qh[:, :LANE].astype(BF16)
        qa_ref[0, :, h * MLA_QK_PAD + LANE:(h + 1) * MLA_QK_PAD] = rope64(qh[:, LANE:]).astype(BF16)

    ckv = cols(C_CKV, MLA_KV_RANK)
    ckvn = ((ckv * rms(ckv, MLA_KV_RANK)) * kvn_g_ref[...]).astype(BF16)
    kn = jnp.dot(ckvn, wkn_ref[...], preferred_element_type=F32)
    va_ref[0] = jnp.dot(ckvn, wv_ref[...], preferred_element_type=F32).astype(BF16)
    krp = cols(C_KR, LANE)
    kr_ssq = jnp.sum(krp * krp, axis=-1, keepdims=True)
    kr_rot = rope64(krp * kgr_ref[...])
    for h in range(MLA_HEADS):
        knh = kn[:, h * LANE:(h + 1) * LANE]
        r = lax.rsqrt((jnp.sum(knh * knh, axis=-1, keepdims=True) + kr_ssq) * (1.0 / MLA_QK) + EPS)
        ka_ref[0, :, h * MLA_QK_PAD:h * MLA_QK_PAD + LANE] = ((knh * r) * kgn_ref[...]).astype(BF16)
        ka_ref[0, :, h * MLA_QK_PAD + LANE:(h + 1) * MLA_QK_PAD] = (kr_rot * r).astype(BF16)

    for h in range(SWA_Q_HEADS):
        t = cols(C_SWA_Q + h * LANE, LANE)
        qb_ref[0, :, h * LANE:(h + 1) * LANE] = rope128((t * rms(t, SWA_HEAD_DIM)) * sqg_ref[...]).astype(BF16)
    for h in range(SWA_KV_HEADS):
        t = cols(C_SWA_K + h * LANE, LANE)
        kb_ref[0, :, h * LANE:(h + 1) * LANE] = rope128((t * rms(t, SWA_HEAD_DIM)) * skg_ref[...]).astype(BF16)

    def seg_norm(t, gain):
        t2 = t * t
        hi = t2.astype(BF16)
        lo = (t2 - hi.astype(F32)).astype(BF16)
        ssq = (jnp.dot(hi, bd_ref[...], preferred_element_type=F32)
               + jnp.dot(lo, bd_ref[...], preferred_element_type=F32))
        return (t * lax.rsqrt(ssq * (1.0 / DIFF_QK_DIM) + EPS)) * gain

    tq = seg_norm(cols(C_DIF_Q, DIFF_WIDTH), dqg_ref[...])
    tk = seg_norm(cols(C_DIF_K, DIFF_WIDTH), dkg_ref[...])
    for j in range(DIFF_WIDTH // LANE):
        sl = slice(j * LANE, (j + 1) * LANE)
        qc_ref[0, :, sl] = rope64(tq[:, sl]).astype(BF16)
        kc_ref[0, :, sl] = rope64(tk[:, sl]).astype(BF16)


def _prep(p, tables, wl, tm):
    bx, tx, _ = p.shape
    rope = tables is not None
    row = lambda a: a.reshape(1, -1)
    full = lambda a: pl.BlockSpec(a.shape, lambda b, i: (0,) * a.ndim)
    consts = [row(wl["mla_q_norm"]), wl["w_uq"], row(wl["q_gain"]), row(wl["mla_kv_norm"]), wl["w_kn"], wl["w_v"],
              row(wl["k_gain_nope"]), row(wl["k_gain_rope"]), row(wl["swa_q_gain"]), row(wl["swa_k_gain"]),
              row(wl["dif_q_gain"]), row(wl["dif_k_gain"]), wl["bd"]]
    ins = [p]
    in_specs = [pl.BlockSpec((1, tm, C_QKV_END), lambda b, i: (b, i, 0))]
    if rope:
        ins += list(tables)
        in_specs += [pl.BlockSpec((tm, LANE), lambda b, i: (i, 0))] * len(tables)
    ins += consts
    in_specs += [full(a) for a in consts]
    widths = [MLA_HEADS * MLA_QK_PAD, MLA_HEADS * MLA_QK_PAD, MLA_WIDTH, SWA_WIDTH,
              SWA_KV_WIDTH, DIFF_WIDTH, DIFF_WIDTH]
    return pl.pallas_call(
        functools.partial(_prep_kernel, rope=rope),
        grid=(bx, tx // tm),
        in_specs=in_specs,
        out_specs=[pl.BlockSpec((1, tm, w), lambda b, i: (b, i, 0)) for w in widths],
        out_shape=[jax.ShapeDtypeStruct((bx, tx, w), BF16) for w in widths],
        compiler_params=_cparams(("parallel", "parallel")),
        name="head_prep_rope" if rope else "head_prep",
    )(*ins)


def _fold_lanes(t, op):
    parts = [t[:, c * LANE:(c + 1) * LANE] for c in range(t.shape[1] // LANE)]
    return functools.reduce(op, parts)


def _key_chunks(srcs, tk):
    out = []
    for k_ref, v_ref in srcs:
        n = k_ref.shape[0]
        for c0 in range(0, n, tk):
            out.append((k_ref, v_ref, c0, min(tk, n - c0)))
    return out


def _dense_attention(q, srcs, s_scr, tk, sink=None):
    chunks = _key_chunks(srcs, tk)
    mpart = None
    off = 0
    for k_ref, _, c0, n in chunks:
        s = lax.dot_general(q, k_ref[c0:c0 + n, :], (((1,), (1,)), ((), ())), preferred_element_type=F32)
        s_scr[:, off:off + n] = s
        f = _fold_lanes(s, jnp.maximum)
        mpart = f if mpart is None else jnp.maximum(mpart, f)
        off += n
    m = jnp.max(mpart, axis=-1, keepdims=True)
    if sink is not None:
        m = jnp.maximum(m, sink)
    lpart = None
    acc = None
    off = 0
    for _, v_ref, c0, n in chunks:
        p = jnp.exp2(s_scr[:, off:off + n] - m)
        f = _fold_lanes(p, jnp.add)
        lpart = f if lpart is None else lpart + f
        pv = jnp.dot(p.astype(BF16), v_ref[c0:c0 + n, :], preferred_element_type=F32)
        acc = pv if acc is None else acc + pv
        off += n
    l = jnp.sum(lpart, axis=-1, keepdims=True)
    if sink is not None:
        l = l + jnp.exp2(sink - m)
    return acc * (1.0 / l)


class _View:
    def __init__(self, ref, lead):
        self.ref, self.lead = ref, lead
        self.shape = ref.shape[len(lead):]

    def __getitem__(self, idx):
        return self.ref[self.lead + idx]


def _score_pass(q, k_views, s_ref, m_ref, tk):
    mpart = None
    off = 0
    for k_ref in k_views:
        n_keys = k_ref.shape[0]
        for c0 in range(0, n_keys, tk):
            n = min(tk, n_keys - c0)
            s = lax.dot_general(q, k_ref[c0:c0 + n, :], (((1,), (1,)), ((), ())), preferred_element_type=F32)
            s_ref[:, off:off + n] = s
            f = _fold_lanes(s, jnp.maximum)
            mpart = f if mpart is None else jnp.maximum(mpart, f)
            off += n
            yield
    m_ref[...] = jnp.broadcast_to(jnp.max(mpart, axis=-1, keepdims=True), m_ref.shape)
    yield


def _prob_pass(s_ref, m_ref, v_views, tk, result):
    mb = m_ref[...]
    lpart = None
    acc = None
    off = 0
    for v_ref in v_views:
        n_keys = v_ref.shape[0]
        for c0 in range(0, n_keys, tk):
            n = min(tk, n_keys - c0)
            ps = []
            for c in range(n // LANE):
                pc = jnp.exp2(s_ref[:, off + c * LANE:off + (c + 1) * LANE] - mb)
                lpart = pc if lpart is None else lpart + pc
                ps.append(pc.astype(BF16))
            pv = jnp.dot(jnp.concatenate(ps, axis=1), v_ref[c0:c0 + n, :], preferred_element_type=F32)
            acc = pv if acc is None else acc + pv
            off += n
            yield
    result.append(acc * (1.0 / jnp.sum(lpart, axis=-1, keepdims=True)))
    yield


def _interleave(*gens):
    live = list(gens)
    while live:
        for g in list(live):
            try:
                next(g)
            except StopIteration:
                live.remove(g)


def _pipelined_attention_kernel(*refs, n_src, n_extra, tk, n_units, load_q, finish):
    q_ref = refs[0]
    k_views = [_View(r, (0,)) for r in refs[1:1 + n_src]]
    v_views = [_View(r, (0,)) for r in refs[1 + n_src:1 + 2 * n_src]]
    extra = refs[1 + 2 * n_src:1 + 2 * n_src + n_extra]
    o_ref, s_a, s_b, m_a, m_b = refs[1 + 2 * n_src + n_extra:]
    t = pl.program_id(0)
    bufs = ((s_a, m_a), (s_b, m_b))

    def score(slot):
        return _score_pass(load_q(q_ref), k_views, bufs[slot][0], bufs[slot][1], tk)

    def prob(slot):
        result = []
        yield from _prob_pass(bufs[slot][0], bufs[slot][1], v_views, tk, result)
        finish(result[0], o_ref, extra)

    @pl.when(t == 0)
    def _():
        _interleave(score(0))

    for parity in range(2):
        @pl.when((t > 0) & (t < n_units) & (t % 2 == parity))
        def _():
            _interleave(score(parity), prob(1 - parity))

    @pl.when(t == n_units)
    def _():
        _interleave(prob((n_units - 1) % 2))


def _pipelined_attention(q, q_width, k_list, v_list, extra, heads, out_width, tq, tk, rows_per_q, load_q, finish, name):
    b, t_q, _ = q.shape
    nq = t_q // tq
    n_units = b * heads * nq

    def unit(t):
        u = jnp.minimum(t, n_units - 1)
        return u // (heads * nq), (u // nq) % heads, u % nq

    def q_map(t):
        bi, h, i = unit(t)
        return bi, i, h

    def o_map(t):
        bi, h, i = unit(jnp.maximum(t - 1, 0))
        return bi, i, h

    def kv_map(col0, lag):
        def f(t):
            bi, h, _ = unit(jnp.maximum(t - lag, 0))
            return bi, 0, col0 + h
        return f

    ins = [q]
    in_specs = [pl.BlockSpec((1, tq, q_width), q_map)]
    total = 0
    for arr, width, col0 in k_list:
        total += arr.shape[1]
        ins.append(arr)
        in_specs.append(pl.BlockSpec((1, arr.shape[1], width), kv_map(col0, 0)))
    for arr, width, col0 in v_list:
        ins.append(arr)
        in_specs.append(pl.BlockSpec((1, arr.shape[1], width), kv_map(col0, 1)))
    ins += list(extra)
    in_specs += [pl.BlockSpec(a.shape, lambda t: (0, 0)) for a in extra]
    m_rows = rows_per_q * tq
    return pl.pallas_call(
        functools.partial(_pipelined_attention_kernel, n_src=len(k_list), n_extra=len(extra), tk=tk,
                          n_units=n_units, load_q=load_q, finish=finish),
        grid=(n_units + 1,),
        in_specs=in_specs,
        out_specs=pl.BlockSpec((1, tq, out_width), o_map),
        out_shape=jax.ShapeDtypeStruct((b, t_q, heads * out_width), BF16),
        scratch_shapes=[pltpu.VMEM((m_rows, total), F32), pltpu.VMEM((m_rows, total), F32),
                        pltpu.VMEM((m_rows, LANE), F32), pltpu.VMEM((m_rows, LANE), F32)],
        compiler_params=_cparams(("arbitrary",)),
        name=name,
    )(*ins)


def _mla_load_q(q_ref):
    return q_ref[0]


def _mla_finish(o, o_ref, extra):
    o_ref[0] = o.astype(BF16)


def _mla_attention(q, kv_list, tq, tk):
    return _pipelined_attention(
        q, MLA_QK_PAD, [(k, MLA_QK_PAD, 0) for k, _ in kv_list], [(v, MLA_V, 0) for _, v in kv_list], (),
        MLA_HEADS, MLA_V, tq, tk, 1, _mla_load_q, _mla_finish, "mla_attention")


def _diff_load_q(q_ref):
    q = q_ref[0]
    first = (lax.broadcasted_iota(jnp.int32, q.shape, 1) & (DIFF_QK_DIM // 2)) == 0
    zero = jnp.zeros_like(q)
    return jnp.concatenate([jnp.where(first, q, zero), jnp.where(first, zero, q)], axis=0)


def _diff_finish(o, o_ref, extra, lam_init):
    lq1_ref, lk1_ref, lq2_ref, lk2_ref, og_ref = extra
    tq = o.shape[0] // 2
    lam = (jnp.exp(jnp.sum(lq1_ref[...] * lk1_ref[...], axis=-1, keepdims=True))
           - jnp.exp(jnp.sum(lq2_ref[...] * lk2_ref[...], axis=-1, keepdims=True)) + lam_init)
    d = o[:tq] - lam * o[tq:]
    r = lax.rsqrt(jnp.mean(d * d, axis=-1, keepdims=True) + EPS)
    o_ref[0] = (((d * r) * og_ref[...]) * (1.0 - lam_init)).astype(BF16)


def _diff_attention(q, kv_list, v_col0, lams, out_gain, lam_init, tq, tk):
    extra = [a.reshape(1, -1) for a in lams] + [out_gain.reshape(1, -1)]
    return _pipelined_attention(
        q, LANE, [(k, LANE, 0) for k, _ in kv_list], [(pv, LANE, v_col0) for _, pv in kv_list], extra,
        DIFF_HEADS, DIFF_V_DIM, tq, tk, 2, _diff_load_q, functools.partial(_diff_finish, lam_init=lam_init),
        "diff_attention")


def _stack_heads(q_ref):
    return jnp.concatenate([q_ref[0, :, g * LANE:(g + 1) * LANE] for g in range(SWA_GROUP)], axis=0)


def _sink_rows(sink_ref, h, tq):
    return jnp.concatenate(
        [jnp.full((tq, 1), sink_ref[h * SWA_GROUP + g] * LOG2E, F32) for g in range(SWA_GROUP)], axis=0)


def _swa_ctx_kernel(sink_ref, q_ref, k_ref, v_ref, o_ref, s_scr):
    tq = q_ref.shape[1]
    o = _dense_attention(_stack_heads(q_ref), [(_View(k_ref, (0,)), _View(v_ref, (0,)))], s_scr, k_ref.shape[1],
                         sink=_sink_rows(sink_ref, pl.program_id(1), tq))
    for g in range(SWA_GROUP):
        o_ref[0, :, g * LANE:(g + 1) * LANE] = o[g * tq:(g + 1) * tq].astype(BF16)


def _swa_window_kernel(sink_ref, q_ref, k_ref, v_ref, kc_ref, vc_ref, o_ref, *, sub):
    i = pl.program_id(1)
    tq = q_ref.shape[1]
    s_len = k_ref.shape[1]
    win = sub + 2 * SWA_WINDOW
    dn = (((1,), (1,)), ((), ()))
    for j in range(tq // sub):
        q0 = i * tq + j * sub
        start = pl.multiple_of(jnp.clip(q0 - SWA_WINDOW, 0, s_len - win), LANE)
        rel = (lax.broadcasted_iota(jnp.int32, (sub, win), 1) - lax.broadcasted_iota(jnp.int32, (sub, win), 0)
               + (start - q0))
        bias1 = jnp.where(jnp.abs(rel) <= SWA_WINDOW, 0.0, NEG_BIG).astype(F32)
        bias = jnp.concatenate([bias1] * SWA_GROUP, axis=0)
        for h in range(SWA_KV_HEADS):
            qq = jnp.concatenate(
                [q_ref[0, j * sub:(j + 1) * sub, (h * SWA_GROUP + g) * LANE:(h * SWA_GROUP + g + 1) * LANE]
                 for g in range(SWA_GROUP)], axis=0)
            hl = slice(h * LANE, (h + 1) * LANE)
            s_w = lax.dot_general(qq, k_ref[0, pl.ds(start, win), hl], dn, preferred_element_type=F32) + bias
            s_c = lax.dot_general(qq, kc_ref[0, :, hl], dn, preferred_element_type=F32)
            sink = _sink_rows(sink_ref, h, sub)
            mf = jnp.maximum(_fold_lanes(s_w, jnp.maximum), _fold_lanes(s_c, jnp.maximum))
            m = jnp.maximum(jnp.max(mf, axis=-1, keepdims=True), sink)
            p_w = jnp.exp2(s_w - m)
            p_c = jnp.exp2(s_c - m)
            lf = _fold_lanes(p_w, jnp.add) + _fold_lanes(p_c, jnp.add)
            l = jnp.sum(lf, axis=-1, keepdims=True) + jnp.exp2(sink - m)
            o = (jnp.dot(p_w.astype(BF16), v_ref[0, pl.ds(start, win), hl], preferred_element_type=F32)
                 + jnp.dot(p_c.astype(BF16), vc_ref[0, :, hl], preferred_element_type=F32)) * (1.0 / l)
            for g in range(SWA_GROUP):
                col = (h * SWA_GROUP + g) * LANE
                o_ref[0, j * sub:(j + 1) * sub, col:col + LANE] = o[g * sub:(g + 1) * sub].astype(BF16)


def _swa_attention(q, k, pv, kc, pvc, sink, tq, sub):
    b, s_len, _ = q.shape
    lc = kc.shape[1]
    vblk = C_SWA_V // SWA_KV_WIDTH
    assert C_SWA_V % SWA_KV_WIDTH == 0
    qspec = pl.BlockSpec((1, tq, SWA_WIDTH), lambda bi, i: (bi, i, 0))
    return pl.pallas_call(
        functools.partial(_swa_window_kernel, sub=sub),
        grid=(b, s_len // tq),
        in_specs=[
            pl.BlockSpec(memory_space=pltpu.SMEM),
            qspec,
            pl.BlockSpec((1, s_len, SWA_KV_WIDTH), lambda bi, i: (bi, 0, 0)),
            pl.BlockSpec((1, s_len, SWA_KV_WIDTH), lambda bi, i: (bi, 0, vblk)),
            pl.BlockSpec((1, lc, SWA_KV_WIDTH), lambda bi, i: (bi, 0, 0)),
            pl.BlockSpec((1, lc, SWA_KV_WIDTH), lambda bi, i: (bi, 0, vblk)),
        ],
        out_specs=qspec,
        out_shape=jax.ShapeDtypeStruct((b, s_len, SWA_WIDTH), BF16),
        compiler_params=_cparams(("parallel", "arbitrary")),
        name="swa_window_attention",
    )(sink, q, k, pv, kc, pvc)


def _swa_ctx_attention(q, k, pv, sink):
    b, lc, _ = q.shape
    vblk = C_SWA_V // LANE
    qspec = pl.BlockSpec((1, lc, SWA_GROUP * LANE), lambda bi, h: (bi, 0, h))
    return pl.pallas_call(
        _swa_ctx_kernel,
        grid=(b, SWA_KV_HEADS),
        in_specs=[
            pl.BlockSpec(memory_space=pltpu.SMEM),
            qspec,
            pl.BlockSpec((1, lc, LANE), lambda bi, h: (bi, 0, h)),
            pl.BlockSpec((1, lc, LANE), lambda bi, h: (bi, 0, vblk + h)),
        ],
        out_specs=qspec,
        out_shape=jax.ShapeDtypeStruct((b, lc, SWA_WIDTH), BF16),
        scratch_shapes=[pltpu.VMEM((SWA_GROUP * lc, lc), F32)],
        compiler_params=_cparams(("parallel", "parallel")),
        name="swa_ctx_attention",
    )(sink, q, k, pv)


def _outproj_kernel(ya_ref, yb_ref, yc_ref, gate_ref, h_ref, g_ref, w_ref, o_ref):
    blk = 2 * LANE
    parts = []
    for y_ref, col0 in ((ya_ref, 0), (yb_ref, MLA_WIDTH), (yc_ref, MLA_WIDTH + SWA_WIDTH)):
        for c in range(0, y_ref.shape[2], blk):
            y = y_ref[0, :, c:c + blk].astype(F32)
            gt = gate_ref[0, :, col0 + c:col0 + c + blk].astype(F32)
            hg = 0.5 * gt
            parts.append((y * (hg + hg * jnp.tanh(hg))).astype(BF16))
    a = jnp.concatenate(parts, axis=-1)
    o_ref[0] = h_ref[0] + g_ref[0] * jnp.dot(a, w_ref[...], preferred_element_type=F32)


def _outproj(ya, yb, yc, p, h, g, w, tm):
    bx, tx, d = h.shape
    tok = lambda wd: pl.BlockSpec((1, tm, wd), lambda b, i: (b, i, 0))
    return pl.pallas_call(
        _outproj_kernel,
        grid=(bx, tx // tm),
        in_specs=[
            tok(MLA_WIDTH), tok(SWA_WIDTH), tok(DIFF_WIDTH),
            pl.BlockSpec((1, tm, MIX_WIDTH), lambda b, i: (b, i, C_GATE // MIX_WIDTH)),
            tok(d),
            pl.BlockSpec((1, 1, d), lambda b, i: (b, 0, 0)),
            pl.BlockSpec(w.shape, lambda b, i: (0, 0)),
        ],
        out_specs=tok(d),
        out_shape=jax.ShapeDtypeStruct((bx, tx, d), F32),
        compiler_params=_cparams(("parallel", "parallel")),
        name="gated_outproj_residual",
    )(ya, yb, yc, p, h, g, w)


def _swap_mid32(a):
    g = a.reshape(a.shape[:-1] + (a.shape[-1] // LANE, 4, LANE // 4))
    return jnp.stack([g[..., 0, :], g[..., 2, :], g[..., 1, :], g[..., 3, :]], axis=-2).reshape(a.shape)


def _layer_weights(l, w_in, mla_w_uq, mla_w_ukv, mla_q_gain, mla_k_gain, w_out):
    d = w_in.shape[1]
    wi = w_in[l]
    o = np.cumsum([0, MLA_Q_RANK, MLA_KV_RANK, MLA_ROPE, MLA_WIDTH, SWA_WIDTH, SWA_KV_WIDTH,
                   SWA_KV_WIDTH, SWA_WIDTH, DIFF_WIDTH, DIFF_WIDTH, DIFF_WIDTH, DIFF_WIDTH]).tolist()
    seg = lambda j: wi[:, o[j]:o[j + 1]]
    z = lambda n: jnp.zeros((d, n), wi.dtype)
    kr_blk = _swap_mid32(jnp.concatenate([seg(2), z(LANE - MLA_ROPE)], axis=1))
    w_p = jnp.concatenate([seg(0), seg(1), kr_blk, z(C_SWA_Q - C_KR - LANE),
                           seg(4), seg(5), seg(6), _swap_mid32(seg(8)), _swap_mid32(seg(9)), seg(10),
                           z(C_QKV_END - C_DIF_V - DIFF_WIDTH),
                           seg(3), seg(7), seg(11)], axis=1).astype(BF16)
    pad = MLA_QK_PAD - MLA_QK
    w_uq = jnp.pad(mla_w_uq[l].reshape(MLA_Q_RANK, MLA_HEADS, MLA_QK), ((0, 0), (0, 0), (0, pad)))
    w_uq = jnp.concatenate([w_uq[..., :LANE], _swap_mid32(w_uq[..., LANE:])], axis=-1)
    q_gain = jnp.pad(mla_q_gain[l], (0, pad))
    w_ukv = mla_w_ukv[l].reshape(MLA_KV_RANK, MLA_HEADS, MLA_NOPE + MLA_V)
    return {
        "w_p": w_p,
        "w_uq": w_uq.reshape(MLA_Q_RANK, MLA_HEADS * MLA_QK_PAD).astype(BF16),
        "w_kn": w_ukv[:, :, :MLA_NOPE].reshape(MLA_KV_RANK, MLA_HEADS * MLA_NOPE).astype(BF16),
        "w_v": w_ukv[:, :, MLA_NOPE:].reshape(MLA_KV_RANK, MLA_WIDTH).astype(BF16),
        "q_gain": jnp.concatenate([q_gain[:LANE], _swap_mid32(q_gain[LANE:])]),
        "k_gain_nope": mla_k_gain[l][:MLA_NOPE],
        "k_gain_rope": _swap_mid32(jnp.pad(mla_k_gain[l][MLA_NOPE:], (0, LANE - MLA_ROPE))),
        "w_out": w_out[l].astype(BF16),
    }


def _rope_tables(n_tokens):
    t = np.arange(n_tokens)
    row, col = (t // GRID_W).astype(np.float64), (t % GRID_W).astype(np.float64)

    def cos_sin(rot_dim):
        n_freq = rot_dim // 4
        inv = np.power(ROPE_BASE, -np.arange(n_freq, dtype=np.float32) / n_freq).astype(np.float32)
        ang = np.concatenate([row[:, None].astype(np.float32) * inv, col[:, None].astype(np.float32) * inv], axis=-1)
        return np.cos(ang).astype(np.float32), np.sin(ang).astype(np.float32)

    c64, s64 = cos_sin(MLA_ROPE)
    c128, s128 = cos_sin(SWA_HEAD_DIM)
    cd = np.concatenate([c64] * 4, axis=-1)
    sd = np.concatenate([-s64, -s64, s64, s64], axis=-1)
    cs = np.concatenate([c128, c128], axis=-1)
    ss = np.concatenate([-s128, s128], axis=-1)
    return tuple(jnp.asarray(a, F32) for a in (cd, sd, cs, ss))


def _segment_ones(n):
    lane = np.arange(n)
    idx = (lane // LANE) * 2 + (lane // (LANE // 4)) % 2
    return jnp.asarray(idx[:, None] == idx[None, :], BF16)


def _pick(n, prefs):
    for t in prefs:
        if n % t == 0:
            return t
    return n


def kernel(x, c, ctx, c_ctx, norm_g, w_ada, b_ada, w_in, mla_q_norm, mla_w_uq, mla_kv_norm, mla_w_ukv,
           mla_q_gain, mla_k_gain, swa_q_gain, swa_k_gain, swa_sink, dif_q_gain, dif_k_gain,
           dif_lq1, dif_lk1, dif_lq2, dif_lk2, dif_out_gain, w_out):
    b, s, d = x.shape
    lc = ctx.shape[1]
    depth = w_in.shape[0]
    assert s % (2 * LANE) == 0 and s >= 4 * LANE and lc % LANE == 0 and d == MIX_WIDTH

    tables = _rope_tables(s)
    bd = _segment_ones(DIFF_WIDTH)

    n_rows = -(-(b + 1) // 8) * 8
    cvec = jnp.concatenate([c, c_ctx[None, :], jnp.zeros((n_rows - b - 1, d), F32)], axis=0)
    mod = _modulation(cvec, w_ada, b_ada)

    hx = x
    hc = ctx.reshape(1, b * lc, d)
    tm_x = _pick(s, (1024, 512, 256))
    tm_c = _pick(b * lc, (1024, 512, 256))
    tq_a = _pick(s, (1024, 512, 256))
    tq_b = _pick(s, (512, 256))
    tq_c = _pick(s, (512, 256))
    tk = 512
    tn = 1536

    for l in range(depth):
        need_ctx_out = l < depth - 1
        wl = _layer_weights(l, w_in, mla_w_uq, mla_w_ukv, mla_q_gain, mla_k_gain, w_out)
        wl.update(mla_q_norm=mla_q_norm[l], mla_kv_norm=mla_kv_norm[l],
                  q_gain=wl["q_gain"] * (MLA_QK ** -0.5 * LOG2E),
                  swa_q_gain=swa_q_gain[l] * (SWA_HEAD_DIM ** -0.5 * LOG2E), swa_k_gain=swa_k_gain[l],
                  dif_q_gain=_swap_mid32(jnp.tile(dif_q_gain[l], DIFF_WIDTH // DIFF_QK_DIM))
                  * (DIFF_QK_DIM ** -0.5 * LOG2E),
                  dif_k_gain=_swap_mid32(jnp.tile(dif_k_gain[l], DIFF_WIDTH // DIFF_QK_DIM)), bd=bd)
        m = mod[l]
        sh_x, sc_x, g_x = (m[:b, j * d:(j + 1) * d].reshape(b, 1, d) for j in range(3))
        sh_c, sc_c, g_c = (m[b:b + 1, j * d:(j + 1) * d].reshape(1, 1, d) for j in range(3))
        ng = norm_g[l].reshape(1, d)
        lam_init = 0.8 - 0.6 * math.exp(-0.3 * l)
        lams = (dif_lq1[l], dif_lk1[l], dif_lq2[l], dif_lk2[l])
        sink = swa_sink[l]

        px = _inproj(hx, sc_x, sh_x, ng, wl["w_p"], tm_x, tn)
        pc = _inproj(hc, sc_c, sh_c, ng, wl["w_p"], tm_c, tn)
        qa_x, ka_x, va_x, qb_x, kb_x, qc_x, kc_x = _prep(px, tables, wl, _pick(s, (512, 256)))
        qa_c, ka_c, va_c, qb_c, kb_c, qc_c, kc_c = (
            t.reshape(b, lc, -1) for t in _prep(pc, None, wl, _pick(b * lc, (512, 256))))
        pcb = pc.reshape(b, lc, P_COLS)
        vblk_c = C_DIF_V // LANE

        ya_x = _mla_attention(qa_x, [(ka_x, va_x), (ka_c, va_c)], tq_a, tk)
        yb_x = _swa_attention(qb_x, kb_x, px, kb_c, pcb, sink, tq_b, 256)
        yc_x = _diff_attention(qc_x, [(kc_x, px), (kc_c, pcb)], vblk_c, lams, dif_out_gain[l], lam_init, tq_c, tk)

        if need_ctx_out:
            ya_c = _mla_attention(qa_c, [(ka_c, va_c)], lc, tk)
            yb_c = _swa_ctx_attention(qb_c, kb_c, pcb, sink)
            yc_c = _diff_attention(qc_c, [(kc_c, pcb)], vblk_c, lams, dif_out_gain[l], lam_init, lc, tk)
            flat = lambda t: t.reshape(1, b * lc, -1)
            hc = _outproj(flat(ya_c), flat(yb_c), flat(yc_c), pc, hc, g_c, wl["w_out"], _pick(b * lc, (512, 256)))

        hx = _outproj(ya_x, yb_x, yc_x, px, hx, g_x, wl["w_out"], _pick(s, (512, 256)))

    return hx
```
